```python
import jax, jax.numpy as jnp
from jax import lax
import numpy as np

D_MODEL = 4096
BATCH = 1
SEQ = 16384
DEPTH = 4

N_MIXERS = 2
N_ATTN_LAYERS = (DEPTH + N_MIXERS - 1) // N_MIXERS
N_POOL_LAYERS = DEPTH // N_MIXERS
HEAD_DIM = 128
N_HEADS = D_MODEL // HEAD_DIM
N_KV = 8
GROUP = N_HEADS // N_KV
Q_W = N_HEADS * HEAD_DIM
KV_W = N_KV * HEAD_DIM
ATTN_IN_W = 2 * Q_W + 2 * KV_W
WINDOW = 128
BLOCK = 128
ATTN_SCALE = HEAD_DIM ** -0.5
POOL_EXPAND = 2
POOL_W = POOL_EXPAND * D_MODEL
POOL_WINDOWS = (2, 4, 8, 16)
N_POOL_GROUPS = len(POOL_WINDOWS)
POOL_GW = POOL_W // N_POOL_GROUPS
PLE_DIM = 256
EPS = 1e-6

kernel_name = "hybrid_swa_pool_ple_encoder"


def rms_norm(x, g):
    xf = x.astype(jnp.float32)
    y = xf * lax.rsqrt(jnp.mean(xf * xf, axis=-1, keepdims=True) + EPS)
    return (y * g.astype(jnp.float32)).astype(x.dtype)


def alibi_slopes():
    return 2.0 ** (-8.0 * jnp.arange(1, N_HEADS + 1, dtype=jnp.float32) / N_HEADS)


def windowed_gqa(h, w_in, q_g, k_g, sink, w_out):
    B, S, _ = h.shape
    u = h @ w_in
    q, k, v, z = jnp.split(u, [Q_W, Q_W + KV_W, Q_W + 2 * KV_W], axis=-1)
    q = rms_norm(q.reshape(B, S, N_KV, GROUP, HEAD_DIM), q_g)
    k = rms_norm(k.reshape(B, S, N_KV, HEAD_DIM), k_g)
    v = v.reshape(B, S, N_KV, HEAD_DIM)
    nb = S // BLOCK
    pad = ((0, 0), (BLOCK, BLOCK), (0, 0), (0, 0))
    kp = jnp.pad(k, pad)
    vp = jnp.pad(v, pad)
    qb = q.reshape(B, nb, BLOCK, N_KV, GROUP, HEAD_DIM).transpose(1, 0, 2, 3, 4, 5)
    slopes = alibi_slopes().reshape(N_KV, GROUP)
    sink_f = sink.astype(jnp.float32).reshape(N_KV, GROUP)

    def band_block(args):
        qi, bi = args
        kw = lax.dynamic_slice_in_dim(kp, bi * BLOCK, 3 * BLOCK, axis=1)
        vw = lax.dynamic_slice_in_dim(vp, bi * BLOCK, 3 * BLOCK, axis=1)
        s = jnp.einsum('btkgd,bskd->bkgts', qi.astype(jnp.float32),
                       kw.astype(jnp.float32)) * ATTN_SCALE
        t_pos = bi * BLOCK + jnp.arange(BLOCK)
        s_pos = bi * BLOCK - BLOCK + jnp.arange(3 * BLOCK)
        dist = jnp.abs(t_pos[:, None] - s_pos[None, :])
        valid = (dist <= WINDOW) & (s_pos >= 0)[None, :] & (s_pos < S)[None, :]
        s = s - slopes[:, :, None, None] * dist.astype(jnp.float32)
        s = jnp.where(valid, s, -jnp.inf)
        sink_col = jnp.broadcast_to(sink_f[None, :, :, None, None], s.shape[:-1] + (1,))
        pr = jax.nn.softmax(jnp.concatenate([s, sink_col], axis=-1), axis=-1)[..., :-1]
        o = jnp.einsum('bkgts,bskd->btkgd', pr, vw.astype(jnp.float32))
        return o.astype(h.dtype)

    o = lax.map(band_block, (qb, jnp.arange(nb)))
    o = o.transpose(1, 0, 2, 3, 4, 5).reshape(B, S, Q_W)
    return (o * jax.nn.silu(z)) @ w_out


def multiscale_pool(h, w_in, w_grp, scale, w_out):
    B, S, _ = h.shape
    u = h @ w_in
    v, z = jnp.split(u, 2, axis=-1)
    vf = v.astype(jnp.float32)
    c = jnp.pad(jnp.cumsum(vf, axis=1), ((0, 0), (1, 0), (0, 0)))
    t = jnp.arange(S)
    outs = []
    for j, w in enumerate(POOL_WINDOWS):
        left = (w - 1) // 2
        right = w - 1 - left
        lo = jnp.clip(t - left, 0, S)
        hi = jnp.clip(t + right + 1, 0, S)
        cj = c[..., j * POOL_GW:(j + 1) * POOL_GW]
        win_sum = jnp.take(cj, hi, axis=1) - jnp.take(cj, lo, axis=1)
        mean = win_sum / (hi - lo).astype(jnp.float32)[None, :, None]
        outs.append(mean - vf[..., j * POOL_GW:(j + 1) * POOL_GW])
    d = jnp.stack(outs, axis=2).astype(h.dtype)
    y = jnp.einsum('bsgc,gcd->bsgd', d, w_grp).reshape(B, S, POOL_W) * scale
    return (y * jax.nn.silu(z)) @ w_out


def per_layer_embed(x, p_i, norm_g, w_gate, w_proj):
    gate = jax.nn.sigmoid((rms_norm(x, norm_g) @ w_gate).astype(jnp.float32))
    return x + (p_i @ w_proj) * gate.astype(x.dtype)


def setup_inputs(seed: int = 0) -> dict:
    key = jax.random.key(seed)
    ks = jax.random.split(key, 16)
    f32 = jnp.float32
    nrm = lambda k, shape, s: jax.random.normal(k, shape, f32) * s
    return {
        "x": nrm(ks[0], (BATCH, SEQ, D_MODEL), 1.0),
        "p": nrm(ks[1], (DEPTH, BATCH, SEQ, PLE_DIM), 1.0),
        "norm_g": 1.0 + nrm(ks[2], (DEPTH, D_MODEL), 0.02),
        "attn_w_in": nrm(ks[3], (N_ATTN_LAYERS, D_MODEL, ATTN_IN_W), D_MODEL ** -0.5),
        "attn_q_norm_g": 1.0 + nrm(ks[4], (N_ATTN_LAYERS, HEAD_DIM), 0.02),
        "attn_k_norm_g": 1.0 + nrm(ks[5], (N_ATTN_LAYERS, HEAD_DIM), 0.02),
        "attn_sink": nrm(ks[6], (N_ATTN_LAYERS, N_HEADS), 0.5),
        "attn_w_out": nrm(ks[7], (N_ATTN_LAYERS, Q_W, D_MODEL), Q_W ** -0.5),
        "pool_w_in": nrm(ks[8], (N_POOL_LAYERS, D_MODEL, 2 * POOL_W), D_MODEL ** -0.5),
        "pool_w_grp": nrm(ks[9], (N_POOL_LAYERS, N_POOL_GROUPS, POOL_GW, POOL_GW), POOL_GW ** -0.5),
        "pool_scale": 1.0 + nrm(ks[10], (N_POOL_LAYERS, POOL_W), 0.1),
        "pool_w_out": nrm(ks[11], (N_POOL_LAYERS, POOL_W, D_MODEL), POOL_W ** -0.5),
        "ple_norm_g": 1.0 + nrm(ks[12], (DEPTH, D_MODEL), 0.02),
        "ple_w_gate": nrm(ks[13], (DEPTH, D_MODEL, D_MODEL), D_MODEL ** -0.5),
        "ple_w_proj": nrm(ks[14], (DEPTH, PLE_DIM, D_MODEL), PLE_DIM ** -0.5),
    }


def reference(x, p, norm_g, attn_w_in, attn_q_norm_g, attn_k_norm_g, attn_sink,
              attn_w_out, pool_w_in, pool_w_grp, pool_scale, pool_w_out,
              ple_norm_g, ple_w_gate, ple_w_proj):
    for i in range(DEPTH):
        h = rms_norm(x, norm_g[i])
        j = i // N_MIXERS
        if i % N_MIXERS == 0:
            x = x + windowed_gqa(h, attn_w_in[j], attn_q_norm_g[j], attn_k_norm_g[j],
                                 attn_sink[j], attn_w_out[j])
        else:
            x = x + multiscale_pool(h, pool_w_in[j], pool_w_grp[j], pool_scale[j],
                                    pool_w_out[j])
        x = per_layer_embed(x, p[i], ple_norm_g[i], ple_w_gate[i], ple_w_proj[i])
    return x
```

```python
import functools

import jax
import jax.numpy as jnp
from jax import lax
from jax.experimental import pallas as pl
from jax.experimental.pallas import tpu as pltpu

F32 = jnp.float32
BF16 = jnp.bfloat16

HEAD_DIM = 128
N_KV = 8
GROUP = 4
N_HEADS = N_KV * GROUP
Q_W = N_HEADS * HEAD_DIM
KV_W = N_KV * HEAD_DIM
WINDOW = 128
BLOCK = 128
ATTN_SCALE = HEAD_DIM ** -0.5
POOL_WINDOWS = (2, 4, 8, 16)
N_POOL_GROUPS = len(POOL_WINDOWS)
EPS = 1e-6
MASKED_SCORE = -1e30

SUBLANES = 8
LANES = 128
VMEM_LIMIT_BYTES = 56 * 1024 * 1024
POOL_HALO = SUBLANES


def _params(*semantics):
    return pltpu.CompilerParams(dimension_semantics=semantics,
                                vmem_limit_bytes=VMEM_LIMIT_BYTES)


def _row_tile(m, want):
    t = min(m, want)
    assert m % t == 0 and t % SUBLANES == 0
    return t


def _mm_tiles(m, k, n):
    if k <= 4096:
        return _row_tile(m, 1024), min(n, 512)
    return _row_tile(m, 512), min(n, 512)


def _rmsnorm_kernel(x_ref, g_ref, o_ref):
    x = x_ref[...]
    ms = jnp.mean(x * x, axis=-1, keepdims=True)
    o_ref[...] = (x * lax.rsqrt(ms + EPS) * g_ref[...]).astype(o_ref.dtype)


def _rmsnorm(x, g):
    m, d = x.shape
    tm = _row_tile(m, 512)
    return pl.pallas_call(
        _rmsnorm_kernel,
        grid=(m // tm,),
        in_specs=[pl.BlockSpec((tm, d), lambda i: (i, 0)),
                  pl.BlockSpec((1, d), lambda i: (0, 0))],
        out_specs=pl.BlockSpec((tm, d), lambda i: (i, 0)),
        out_shape=jax.ShapeDtypeStruct((m, d), BF16),
        compiler_params=_params("parallel"),
        name="rmsnorm",
    )(x, g.reshape(1, d))


def _dot(a, b):
    return jnp.dot(a, b, preferred_element_type=F32)


def _mm_kernel(a_ref, w_ref, o_ref):
    o_ref[...] = _dot(a_ref[...], w_ref[...]).astype(o_ref.dtype)


def _matmul(a, w, out_dtype):
    m, k = a.shape
    n = w.shape[1]
    tm, tn = _mm_tiles(m, k, n)
    return pl.pallas_call(
        _mm_kernel,
        grid=(m // tm, n // tn),
        in_specs=[pl.BlockSpec((tm, k), lambda i, j: (i, 0)),
                  pl.BlockSpec((k, tn), lambda i, j: (0, j))],
        out_specs=pl.BlockSpec((tm, tn), lambda i, j: (i, j)),
        out_shape=jax.ShapeDtypeStruct((m, n), out_dtype),
        compiler_params=_params("parallel", "arbitrary"),
        name="matmul",
    )(a, w)


def _mm_headnorm_kernel(a_ref, w_ref, g_ref, o_ref, *, norm_tiles):
    acc = _dot(a_ref[...], w_ref[...])
    j = pl.program_id(1)

    @pl.when(j < norm_tiles)
    def _():
        tn = acc.shape[1]
        for h in range(tn // HEAD_DIM):
            sl = slice(h * HEAD_DIM, (h + 1) * HEAD_DIM)
            a = acc[:, sl]
            ms = jnp.mean(a * a, axis=-1, keepdims=True)
            o_ref[:, sl] = (a * lax.rsqrt(ms + EPS) * g_ref[:, sl]).astype(o_ref.dtype)

    @pl.when(j >= norm_tiles)
    def _():
        o_ref[...] = acc.astype(o_ref.dtype)


def _matmul_headnorm(a, w, g_cols, norm_cols):
    m, k = a.shape
    n = w.shape[1]
    tm, tn = _mm_tiles(m, k, n)
    assert norm_cols % tn == 0
    return pl.pallas_call(
        functools.partial(_mm_headnorm_kernel, norm_tiles=norm_cols // tn),
        grid=(m // tm, n // tn),
        in_specs=[pl.BlockSpec((tm, k), lambda i, j: (i, 0)),
                  pl.BlockSpec((k, tn), lambda i, j: (0, j)),
                  pl.BlockSpec((1, tn), lambda i, j: (0, j))],
        out_specs=pl.BlockSpec((tm, tn), lambda i, j: (i, j)),
        out_shape=jax.ShapeDtypeStruct((m, n), BF16),
        compiler_params=_params("parallel", "arbitrary"),
        name="matmul_headnorm",
    )(a, w, g_cols)


def _mm_residual_kernel(a_ref, w_ref, r_ref, o_ref):
    o_ref[...] = r_ref[...] + _dot(a_ref[...], w_ref[...])


def _matmul_residual(a, w, r):
    m, k = a.shape
    n = w.shape[1]
    tm, tn = _mm_tiles(m, k, n)
    return pl.pallas_call(
        _mm_residual_kernel,
        grid=(m // tm, n // tn),
        in_specs=[pl.BlockSpec((tm, k), lambda i, j: (i, 0)),
                  pl.BlockSpec((k, tn), lambda i, j: (0, j)),
                  pl.BlockSpec((tm, tn), lambda i, j: (i, j))],
        out_specs=pl.BlockSpec((tm, tn), lambda i, j: (i, j)),
        out_shape=jax.ShapeDtypeStruct((m, n), F32),
        compiler_params=_params("parallel", "arbitrary"),
        name="matmul_residual",
    )(a, w, r)


def _ple_kernel(h_ref, wg_ref, p_ref, wp_ref, x_ref, o_ref):
    gate = jax.nn.sigmoid(_dot(h_ref[...], wg_ref[...]))
    proj = _dot(p_ref[...].astype(BF16), wp_ref[...])
    o_ref[...] = x_ref[...] + proj * gate


def _per_layer_embed(x, h, p, w_gate, w_proj):
    m, k = h.shape
    n = w_gate.shape[1]
    kp = p.shape[1]
    tm, tn = _mm_tiles(m, k, n)
    return pl.pallas_call(
        _ple_kernel,
        grid=(m // tm, n // tn),
        in_specs=[pl.BlockSpec((tm, k), lambda i, j: (i, 0)),
                  pl.BlockSpec((k, tn), lambda i, j: (0, j)),
                  pl.BlockSpec((tm, kp), lambda i, j: (i, 0)),
                  pl.BlockSpec((kp, tn), lambda i, j: (0, j)),
                  pl.BlockSpec((tm, tn), lambda i, j: (i, j))],
        out_specs=pl.BlockSpec((tm, tn), lambda i, j: (i, j)),
        out_shape=jax.ShapeDtypeStruct((m, n), F32),
        compiler_params=_params("parallel", "arbitrary"),
        name="per_layer_embed",
    )(h, w_gate, p, w_proj, x)


def _attn_kernel(slope_ref, sink_ref, q_ref, kp_ref, kc_ref, kn_ref,
                 vp_ref, vc_ref, vn_ref, z_ref, o_ref, *, seq_len):
    bi = pl.program_id(0)
    kv = pl.program_id(1)
    q4 = jnp.concatenate(
        [q_ref[:, g * HEAD_DIM:(g + 1) * HEAD_DIM] for g in range(GROUP)], axis=0)
    kw = jnp.concatenate([kp_ref[...], kc_ref[...], kn_ref[...]], axis=0)
    vw = jnp.concatenate([vp_ref[...], vc_ref[...], vn_ref[...]], axis=0)
    s = lax.dot_general(q4, kw, (((1,), (1,)), ((), ())),
                        preferred_element_type=F32) * ATTN_SCALE
    t = lax.broadcasted_iota(jnp.int32, (BLOCK, 3 * BLOCK), 0)
    c = lax.broadcasted_iota(jnp.int32, (BLOCK, 3 * BLOCK), 1)
    dist = jnp.abs(t + BLOCK - c)
    s_pos = (bi - 1) * BLOCK + c
    valid = (dist <= WINDOW) & (s_pos >= 0) & (s_pos < seq_len)
    dist_f = dist.astype(F32)
    for g in range(GROUP):
        head = kv * GROUP + g
        sg = s[g * BLOCK:(g + 1) * BLOCK] - slope_ref[head] * dist_f
        sg = jnp.where(valid, sg, MASKED_SCORE)
        sink = sink_ref[head]
        mx = jnp.maximum(jnp.max(sg, axis=-1, keepdims=True), sink)
        e = jnp.exp(sg - mx)
        denom = jnp.sum(e, axis=-1, keepdims=True) + jnp.exp(sink - mx)
        pr = e * (1.0 / denom)
        og = _dot(pr.astype(BF16), vw)
        sl = slice(g * HEAD_DIM, (g + 1) * HEAD_DIM)
        z = z_ref[:, sl]
        o_ref[:, sl] = (og * (z * jax.nn.sigmoid(z))).astype(o_ref.dtype)


def _attention(qkv, z, slopes, sink):
    s_len = qkv.shape[0]
    nb = s_len // BLOCK
    k0 = Q_W // HEAD_DIM
    v0 = (Q_W + KV_W) // HEAD_DIM
    gw = GROUP * HEAD_DIM

    def prev(bi):
        return jnp.maximum(bi - 1, 0)

    def nxt(bi):
        return jnp.minimum(bi + 1, nb - 1)

    kv_block = (BLOCK, HEAD_DIM)
    smem = pl.BlockSpec(memory_space=pltpu.SMEM)
    return pl.pallas_call(
        functools.partial(_attn_kernel, seq_len=s_len),
        grid=(nb, N_KV),
        in_specs=[smem, smem,
                  pl.BlockSpec((BLOCK, gw), lambda bi, kv: (bi, kv)),
                  pl.BlockSpec(kv_block, lambda bi, kv: (prev(bi), k0 + kv)),
                  pl.BlockSpec(kv_block, lambda bi, kv: (bi, k0 + kv)),
                  pl.BlockSpec(kv_block, lambda bi, kv: (nxt(bi), k0 + kv)),
                  pl.BlockSpec(kv_block, lambda bi, kv: (prev(bi), v0 + kv)),
                  pl.BlockSpec(kv_block, lambda bi, kv: (bi, v0 + kv)),
                  pl.BlockSpec(kv_block, lambda bi, kv: (nxt(bi), v0 + kv)),
                  pl.BlockSpec((BLOCK, gw), lambda bi, kv: (bi, kv))],
        out_specs=pl.BlockSpec((BLOCK, gw), lambda bi, kv: (bi, kv)),
        out_shape=jax.ShapeDtypeStruct((s_len, Q_W), BF16),
        compiler_params=_params("parallel", "arbitrary"),
        name="banded_attention",
    )(slopes, sink, qkv, qkv, qkv, qkv, qkv, qkv, qkv, z)


def _pool_group_kernel(vp_ref, vc_ref, vn_ref, w_ref, scale_ref, z_ref, o_ref,
                       ext_ref, d_ref, *, seq_len):
    g = pl.program_id(0)
    i = pl.program_id(1)
    n_i = pl.num_programs(1)
    tm = vc_ref.shape[0]
    ext_ref[0:POOL_HALO, :] = jnp.where(i > 0, vp_ref[...], 0.0)
    ext_ref[POOL_HALO:POOL_HALO + tm, :] = vc_ref[...]
    ext_ref[POOL_HALO + tm:, :] = jnp.where(i < n_i - 1, vn_ref[...], 0.0)
    t = i * tm + lax.broadcasted_iota(jnp.int32, (tm, 1), 0)

    for grp, win in enumerate(POOL_WINDOWS):
        left = (win - 1) // 2
        right = win - 1 - left

        @pl.when(g == grp)
        def _(left=left, right=right):
            lo = jnp.maximum(t - left, 0)
            hi = jnp.minimum(t + right + 1, seq_len)
            inv_cnt = 1.0 / (hi - lo).astype(F32)
            acc = ext_ref[pl.ds(POOL_HALO - left, tm), :]
            for off in range(-left + 1, right + 1):
                acc = acc + ext_ref[pl.ds(POOL_HALO + off, tm), :]
            d_ref[...] = (acc * inv_cnt - vc_ref[...]).astype(d_ref.dtype)

    y = _dot(d_ref[...], w_ref[0]) * scale_ref[...]
    z = z_ref[...]
    o_ref[...] = (y * (z * jax.nn.sigmoid(z))).astype(o_ref.dtype)


def _pool_group(u, w_grp, scale):
    s_len = u.shape[0]
    gw = w_grp.shape[1]
    tm = _row_tile(s_len, 512)
    n_i = s_len // tm
    halo_per_tile = tm // POOL_HALO
    n_halo = s_len // POOL_HALO
    return pl.pallas_call(
        functools.partial(_pool_group_kernel, seq_len=s_len),
        grid=(N_POOL_GROUPS, n_i),
        in_specs=[
            pl.BlockSpec((POOL_HALO, gw),
                         lambda g, i: (jnp.maximum(i * halo_per_tile - 1, 0), g)),
            pl.BlockSpec((tm, gw), lambda g, i: (i, g)),
            pl.BlockSpec((POOL_HALO, gw),
                         lambda g, i: (jnp.minimum((i + 1) * halo_per_tile, n_halo - 1), g)),
            pl.BlockSpec((1, gw, gw), lambda g, i: (g, 0, 0)),
            pl.BlockSpec((1, gw), lambda g, i: (0, g)),
            pl.BlockSpec((tm, gw), lambda g, i: (i, N_POOL_GROUPS + g)),
        ],
        out_specs=pl.BlockSpec((tm, gw), lambda g, i: (i, g)),
        out_shape=jax.ShapeDtypeStruct((s_len, N_POOL_GROUPS * gw), BF16),
        scratch_shapes=[pltpu.VMEM((tm + 2 * POOL_HALO, gw), F32),
                        pltpu.VMEM((tm, gw), BF16)],
        compiler_params=_params("arbitrary", "arbitrary"),
        name="pool_group",
    )(u, u, u, w_grp, scale.reshape(1, -1), u)


def _alibi_slopes():
    return 2.0 ** (-8.0 * jnp.arange(1, N_HEADS + 1, dtype=F32) / N_HEADS)


def _attention_layer(x, norm_g, w_in, q_g, k_g, sink, w_out):
    h = _rmsnorm(x, norm_g)
    w_in = w_in.astype(BF16)
    qkv_w = Q_W + 2 * KV_W
    g_cols = jnp.concatenate([jnp.tile(q_g.astype(F32), N_HEADS),
                              jnp.tile(k_g.astype(F32), N_KV),
                              jnp.ones((KV_W,), F32)]).reshape(1, qkv_w)
    qkv = _matmul_headnorm(h, w_in[:, :qkv_w], g_cols, Q_W + KV_W)
    z = _matmul(h, w_in[:, qkv_w:], F32)
    gated = _attention(qkv, z, _alibi_slopes(), sink.astype(F32))
    return _matmul_residual(gated, w_out.astype(BF16), x)


def _pool_layer(x, norm_g, w_in, w_grp, scale, w_out):
    h = _rmsnorm(x, norm_g)
    u = _matmul(h, w_in.astype(BF16), F32)
    gated = _pool_group(u, w_grp.astype(BF16), scale)
    return _matmul_residual(gated, w_out.astype(BF16), x)


def kernel(x, p, norm_g, attn_w_in, attn_q_norm_g, attn_k_norm_g, attn_sink, attn_w_out,
           pool_w_in, pool_w_grp, pool_scale, pool_w_out, ple_norm_g, ple_w_gate, ple_w_proj):
    b, s_len, d = x.shape
    assert b == 1
    depth = p.shape[0]
    xf = x.reshape(s_len, d)
    for i in range(depth):
        j = i // 2
        if i % 2 == 0:
            xf = _attention_layer(xf, norm_g[i], attn_w_in[j], attn_q_norm_g[j],
                                  attn_k_norm_g[j], attn_sink[j], attn_w_out[j])
        else:
            xf = _pool_layer(xf, norm_g[i], pool_w_in[j], pool_w_grp[j], pool_scale[j],
                             pool_w_out[j])
        h = _rmsnorm(xf, ple_norm_g[i])
        xf = _per_layer_embed(xf, h, p[i].reshape(s_len, -1),
                              ple_w_gate[i].astype(BF16), ple_w_proj[i].astype(BF16))
    return xf.reshape(b, s_len, d)
```

```python
import functools
import math

import jax
import jax.numpy as jnp
from jax import lax
from jax.experimental import pallas as pl
from jax.experimental.pallas import tpu as pltpu

F32 = jnp.float32
BF16 = jnp.bfloat16

HEAD_DIM = 128
N_KV = 8
GROUP = 4
N_HEADS = N_KV * GROUP
Q_W = N_HEADS * HEAD_DIM
KV_W = N_KV * HEAD_DIM
QKV_W = Q_W + 2 * KV_W
WINDOW = 128
BLOCK = 128
ATTN_SCALE = HEAD_DIM ** -0.5
POOL_WINDOWS = (2, 4, 8, 16)
N_POOL_GROUPS = len(POOL_WINDOWS)
EPS = 1e-6
LOG2E = math.log2(math.e)
MASKED_SCORE = -1e30

SUBLANES = 8
LANES = 128
VMEM_LIMIT_BYTES = 58 * 1024 * 1024
POOL_HALO = SUBLANES

ROW_TILE = 1024
COL_TILE = 512
COL_TILE_WIDE_K = 256
POOL_ROW_TILE = 512
POOL_CHUNK = 128
PREP_ROW_TILE = 512


def _params(*semantics):
    return pltpu.CompilerParams(dimension_semantics=semantics,
                                vmem_limit_bytes=VMEM_LIMIT_BYTES)


def _row_tile(m, want):
    t = min(m, want)
    assert m % t == 0 and t % SUBLANES == 0
    return t


def _dot(a, b):
    return jnp.dot(a, b, preferred_element_type=F32)


def _resident(block_shape, index_map):
    return pl.BlockSpec(block_shape, index_map, pipeline_mode=pl.Buffered(1))


def _emit_norm_inputs(x_new, gain_ref, xg_ref, rinv_ref, d_model):
    j = pl.program_id(1)
    xg_ref[...] = (x_new * gain_ref[...]).astype(xg_ref.dtype)
    part = jnp.sum(x_new * x_new, axis=-1, keepdims=True)

    @pl.when(j == 0)
    def _():
        rinv_ref[...] = part

    @pl.when(j > 0)
    def _():
        rinv_ref[...] += part

    @pl.when(j == pl.num_programs(1) - 1)
    def _():
        rinv_ref[...] = lax.rsqrt(rinv_ref[...] * (1.0 / d_model) + EPS)


def _prep_kernel(x_ref, g_ref, xg_ref, rinv_ref):
    x = x_ref[...]
    xg_ref[...] = (x * g_ref[...]).astype(xg_ref.dtype)
    rinv_ref[...] = lax.rsqrt(jnp.mean(x * x, axis=-1, keepdims=True) + EPS)


def _prep(x, g):
    m, d = x.shape
    tm = _row_tile(m, PREP_ROW_TILE)
    return pl.pallas_call(
        _prep_kernel,
        grid=(m // tm,),
        in_specs=[pl.BlockSpec((tm, d), lambda i: (i, 0)),
                  pl.BlockSpec((1, d), lambda i: (0, 0))],
        out_specs=[pl.BlockSpec((tm, d), lambda i: (i, 0)),
                   pl.BlockSpec((tm, 1), lambda i: (i, 0))],
        out_shape=[jax.ShapeDtypeStruct((m, d), BF16),
                   jax.ShapeDtypeStruct((m, 1), F32)],
        compiler_params=_params("parallel"),
        name="norm_prep",
    )(x, g.reshape(1, d))


def _normed_dot(a_ref, rinv_ref, w_ref):
    return _dot(a_ref[...], w_ref[...].astype(BF16)) * rinv_ref[...]


def _mm_kernel(a_ref, rinv_ref, w_ref, o_ref):
    o_ref[...] = _normed_dot(a_ref, rinv_ref, w_ref).astype(o_ref.dtype)


def _matmul_normed(a, rinv, w, layer, col0, n, out_dtype, name):
    m, k = a.shape
    tm, tn = _row_tile(m, ROW_TILE), COL_TILE
    assert n % tn == 0 and col0 % tn == 0
    j0 = col0 // tn
    return pl.pallas_call(
        _mm_kernel,
        grid=(m // tm, n // tn),
        in_specs=[pl.BlockSpec((tm, k), lambda i, j: (i, 0)),
                  pl.BlockSpec((tm, 1), lambda i, j: (i, 0)),
                  pl.BlockSpec((None, k, tn), lambda i, j: (layer, 0, j0 + j))],
        out_specs=pl.BlockSpec((tm, tn), lambda i, j: (i, j)),
        out_shape=jax.ShapeDtypeStruct((m, n), out_dtype),
        compiler_params=_params("parallel", "arbitrary"),
        name=name,
    )(a, rinv, w)


def _mm_headnorm_kernel(a_ref, rinv_ref, w_ref, g_ref, o_ref, *, norm_tiles):
    acc = _normed_dot(a_ref, rinv_ref, w_ref)
    j = pl.program_id(1)

    @pl.when(j < norm_tiles)
    def _():
        for h in range(acc.shape[1] // HEAD_DIM):
            sl = slice(h * HEAD_DIM, (h + 1) * HEAD_DIM)
            a = acc[:, sl]
            ms = jnp.mean(a * a, axis=-1, keepdims=True)
            o_ref[:, sl] = (a * lax.rsqrt(ms + EPS) * g_ref[:, sl]).astype(o_ref.dtype)

    @pl.when(j >= norm_tiles)
    def _():
        o_ref[...] = acc.astype(o_ref.dtype)


def _matmul_headnorm(a, rinv, w, layer, g_cols, n, norm_cols):
    m, k = a.shape
    tm, tn = _row_tile(m, ROW_TILE), COL_TILE
    assert n % tn == 0 and norm_cols % tn == 0
    return pl.pallas_call(
        functools.partial(_mm_headnorm_kernel, norm_tiles=norm_cols // tn),
        grid=(m // tm, n // tn),
        in_specs=[pl.BlockSpec((tm, k), lambda i, j: (i, 0)),
                  pl.BlockSpec((tm, 1), lambda i, j: (i, 0)),
                  pl.BlockSpec((None, k, tn), lambda i, j: (layer, 0, j)),
                  pl.BlockSpec((1, tn), lambda i, j: (0, j))],
        out_specs=pl.BlockSpec((tm, tn), lambda i, j: (i, j)),
        out_shape=jax.ShapeDtypeStruct((m, n), BF16),
        compiler_params=_params("parallel", "arbitrary"),
        name="matmul_headnorm",
    )(a, rinv, w, g_cols)


def _mm_residual_kernel(*refs, n_a):
    a_refs = refs[:n_a]
    w_ref, res_ref, gain_ref, x_ref, xg_ref, rinv_ref = refs[n_a:]
    kc = a_refs[0].shape[1]
    acc = res_ref[...]
    for c, a_ref in enumerate(a_refs):
        acc = acc + _dot(a_ref[...], w_ref[c * kc:(c + 1) * kc, :].astype(BF16))
    x_ref[...] = acc
    _emit_norm_inputs(acc, gain_ref, xg_ref, rinv_ref, x_ref.shape[1] * pl.num_programs(1))


def _matmul_residual(a_list, w, layer, res, next_gain):
    m, kc = a_list[0].shape
    _, k, n = w.shape
    n_a = len(a_list)
    assert kc * n_a == k
    tm = _row_tile(m, ROW_TILE)
    tn = COL_TILE if k <= 4096 else COL_TILE_WIDE_K
    a_spec = pl.BlockSpec((tm, kc), lambda i, j: (i, 0)) if k <= 4096 else \
        _resident((tm, kc), lambda i, j: (i, 0))
    tile = pl.BlockSpec((tm, tn), lambda i, j: (i, j))
    return pl.pallas_call(
        functools.partial(_mm_residual_kernel, n_a=n_a),
        grid=(m // tm, n // tn),
        in_specs=[a_spec] * n_a + [pl.BlockSpec((None, k, tn), lambda i, j: (layer, 0, j)),
                                   tile,
                                   pl.BlockSpec((1, tn), lambda i, j: (0, j))],
        out_specs=[tile, tile, pl.BlockSpec((tm, 1), lambda i, j: (i, 0))],
        out_shape=[jax.ShapeDtypeStruct((m, n), F32),
                   jax.ShapeDtypeStruct((m, n), BF16),
                   jax.ShapeDtypeStruct((m, 1), F32)],
        compiler_params=_params("parallel", "arbitrary"),
        name="matmul_residual",
    )(*a_list, w, res, next_gain.reshape(1, n))


def _ple_update(a_ref, rinv_ref, wg_ref, p_ref, wp_ref, res_ref):
    gate = jax.nn.sigmoid(_normed_dot(a_ref, rinv_ref, wg_ref))
    proj = _dot(p_ref[...].astype(BF16), wp_ref[...].astype(BF16))
    return res_ref[...] + proj * gate


def _ple_kernel(a_ref, rinv_ref, wg_ref, p_ref, wp_ref, res_ref, gain_ref,
                x_ref, xg_ref, rinv_out_ref):
    x_new = _ple_update(a_ref, rinv_ref, wg_ref, p_ref, wp_ref, res_ref)
    x_ref[...] = x_new
    _emit_norm_inputs(x_new, gain_ref, xg_ref, rinv_out_ref,
                      x_ref.shape[1] * pl.num_programs(1))


def _ple_last_kernel(a_ref, rinv_ref, wg_ref, p_ref, wp_ref, res_ref, x_ref):
    x_ref[...] = _ple_update(a_ref, rinv_ref, wg_ref, p_ref, wp_ref, res_ref)


def _per_layer_embed(x, xg, rinv, p, w_gate, w_proj, layer, next_gain):
    m, k = xg.shape
    n = w_gate.shape[2]
    kp = p.shape[3]
    tm, tn = _row_tile(m, ROW_TILE), COL_TILE
    tile = pl.BlockSpec((tm, tn), lambda i, j: (i, j))
    col = pl.BlockSpec((tm, 1), lambda i, j: (i, 0))
    in_specs = [pl.BlockSpec((tm, k), lambda i, j: (i, 0)),
                col,
                pl.BlockSpec((None, k, tn), lambda i, j: (layer, 0, j)),
                _resident((None, None, tm, kp), lambda i, j: (layer, 0, i, 0)),
                pl.BlockSpec((None, kp, tn), lambda i, j: (layer, 0, j)),
                tile]
    operands = [xg, rinv, w_gate, p, w_proj, x]
    x_shape = jax.ShapeDtypeStruct((m, n), F32)
    if next_gain is None:
        x_new = pl.pallas_call(
            _ple_last_kernel, grid=(m // tm, n // tn), in_specs=in_specs, out_specs=tile,
            out_shape=x_shape, compiler_params=_params("parallel", "arbitrary"),
            name="per_layer_embed_last",
        )(*operands)
        return x_new, None, None
    return pl.pallas_call(
        _ple_kernel,
        grid=(m // tm, n // tn),
        in_specs=in_specs + [pl.BlockSpec((1, tn), lambda i, j: (0, j))],
        out_specs=[tile, tile, col],
        out_shape=[x_shape,
                   jax.ShapeDtypeStruct((m, n), BF16),
                   jax.ShapeDtypeStruct((m, 1), F32)],
        compiler_params=_params("parallel", "arbitrary"),
        name="per_layer_embed",
    )(*operands, next_gain.reshape(1, n))


def _attn_heads(sink_ref, bias_ref, q_ref, kp_ref, kc_ref, kn_ref, vp_ref, vc_ref, vn_ref,
                z_ref, o_ref, p_ref, *, outside):
    qscale = ATTN_SCALE * LOG2E
    for kv in range(N_KV):
        ksl = slice(kv * HEAD_DIM, (kv + 1) * HEAD_DIM)
        q4 = jnp.concatenate(
            [q_ref[:, (kv * GROUP + g) * HEAD_DIM:(kv * GROUP + g + 1) * HEAD_DIM]
             for g in range(GROUP)], axis=0)
        kw = jnp.concatenate([kp_ref[:, ksl], kc_ref[:, ksl], kn_ref[:, ksl]], axis=0)
        vw = jnp.concatenate([vp_ref[:, ksl], vc_ref[:, ksl], vn_ref[:, ksl]], axis=0)
        s = lax.dot_general(q4, kw, (((1,), (1,)), ((), ())), preferred_element_type=F32)
        for g in range(GROUP):
            head = kv * GROUP + g
            sg = s[g * BLOCK:(g + 1) * BLOCK] * qscale + bias_ref[head]
            if outside is not None:
                sg = jnp.where(outside, MASKED_SCORE, sg)
            sink2 = sink_ref[head] * LOG2E
            mx = jnp.maximum(jnp.max(sg, axis=-1, keepdims=True), sink2)
            e = jnp.exp2(sg - mx)
            denom = jnp.sum(e, axis=-1, keepdims=True) + jnp.exp2(sink2 - mx)
            p_ref[g * BLOCK:(g + 1) * BLOCK, :] = (e * (1.0 / denom)).astype(p_ref.dtype)
        o4 = _dot(p_ref[...], vw)
        for g in range(GROUP):
            sl = slice((kv * GROUP + g) * HEAD_DIM, (kv * GROUP + g + 1) * HEAD_DIM)
            z = z_ref[:, sl]
            o_ref[:, sl] = (o4[g * BLOCK:(g + 1) * BLOCK]
                            * (z * jax.nn.sigmoid(z))).astype(o_ref.dtype)


def _attn_kernel(*refs):
    bi = pl.program_id(0)
    is_first = bi == 0
    is_last = bi == pl.num_programs(0) - 1
    on_edge = is_first | is_last

    @pl.when(jnp.logical_not(on_edge))
    def _():
        _attn_heads(*refs, outside=None)

    @pl.when(on_edge)
    def _():
        col = lax.broadcasted_iota(jnp.int32, (BLOCK, 3 * BLOCK), 1)
        outside = (is_first & (col < BLOCK)) | (is_last & (col >= 2 * BLOCK))
        _attn_heads(*refs, outside=outside)


def _attention_bias():
    slopes = 2.0 ** (-8.0 * jnp.arange(1, N_HEADS + 1, dtype=F32) / N_HEADS)
    t = jnp.arange(BLOCK)[:, None]
    c = jnp.arange(3 * BLOCK)[None, :]
    dist = jnp.abs(t + BLOCK - c)
    bias = -(slopes * LOG2E)[:, None, None] * dist.astype(F32)[None]
    return jnp.where((dist <= WINDOW)[None], bias, MASKED_SCORE)


def _attention(qkv, z, sink):
    s_len = qkv.shape[0]
    nb = s_len // BLOCK
    k_blk = Q_W // KV_W
    v_blk = k_blk + 1

    def prev(bi):
        return jnp.maximum(bi - 1, 0)

    def nxt(bi):
        return jnp.minimum(bi + 1, nb - 1)

    kv_block = (BLOCK, KV_W)
    q_spec = pl.BlockSpec((BLOCK, Q_W), lambda bi: (bi, 0))
    return pl.pallas_call(
        _attn_kernel,
        grid=(nb,),
        in_specs=[pl.BlockSpec(memory_space=pltpu.SMEM),
                  _resident((N_HEADS, BLOCK, 3 * BLOCK), lambda bi: (0, 0, 0)),
                  q_spec,
                  pl.BlockSpec(kv_block, lambda bi: (prev(bi), k_blk)),
                  pl.BlockSpec(kv_block, lambda bi: (bi, k_blk)),
                  pl.BlockSpec(kv_block, lambda bi: (nxt(bi), k_blk)),
                  pl.BlockSpec(kv_block, lambda bi: (prev(bi), v_blk)),
                  pl.BlockSpec(kv_block, lambda bi: (bi, v_blk)),
                  pl.BlockSpec(kv_block, lambda bi: (nxt(bi), v_blk)),
                  q_spec],
        out_specs=q_spec,
        out_shape=jax.ShapeDtypeStruct((s_len, Q_W), BF16),
        scratch_shapes=[pltpu.VMEM((GROUP * BLOCK, 3 * BLOCK), BF16)],
        compiler_params=_params("arbitrary"),
        name="banded_attention",
    )(sink, _attention_bias(), qkv, qkv, qkv, qkv, qkv, qkv, qkv, z)


def _window_sum(e, win, rows):
    n = e.shape[0]
    h = POOL_HALO

    def ahead(x, k):
        return pltpu.roll(x, n - k, axis=0)

    def behind(x, k):
        return pltpu.roll(x, k, axis=0)

    if win == 2:
        return (e + ahead(e, 1))[h:h + rows]
    q1 = e + behind(e, 1)
    if win == 4:
        return (q1 + ahead(q1, 2))[h:h + rows]
    q2 = q1 + behind(q1, 2)
    if win == 8:
        return (q2 + ahead(q2, 4))[h:h + rows]
    assert win == 16
    q3 = q2 + behind(q2, 4)
    return q3[h:h + rows] + q3[2 * h:2 * h + rows]


def _pool_group_kernel(vp_ref, vc_ref, vn_ref, w_ref, scale_ref, z_ref, o_ref, d_ref,
                       *, seq_len, win):
    i = pl.program_id(0)
    n_i = pl.num_programs(0)
    tm = vc_ref.shape[0]
    left = (win - 1) // 2
    right = win - 1 - left
    h = POOL_HALO
    n_chunks = tm // POOL_CHUNK
    for c in range(n_chunks):
        r0 = c * POOL_CHUNK
        top = jnp.where(i > 0, vp_ref[...], 0.0) if c == 0 else vc_ref[r0 - h:r0, :]
        bot = (jnp.where(i < n_i - 1, vn_ref[...], 0.0) if c == n_chunks - 1
               else vc_ref[r0 + POOL_CHUNK:r0 + POOL_CHUNK + h, :])
        cur = vc_ref[r0:r0 + POOL_CHUNK, :]
        e = jnp.concatenate([top, cur, bot], axis=0)
        t = i * tm + r0 + lax.broadcasted_iota(jnp.int32, (POOL_CHUNK, 1), 0)
        lo = jnp.maximum(t - left, 0)
        hi = jnp.minimum(t + right + 1, seq_len)
        inv_cnt = 1.0 / (hi - lo).astype(F32)
        d_ref[r0:r0 + POOL_CHUNK, :] = (
            _window_sum(e, win, POOL_CHUNK) * inv_cnt - cur).astype(d_ref.dtype)
    y = _dot(d_ref[...], w_ref[...].astype(BF16)) * scale_ref[...]
    z = z_ref[...]
    o_ref[...] = (y * (z * jax.nn.sigmoid(z))).astype(o_ref.dtype)


def _pool_group(u, w_grp, layer, scale, grp):
    s_len = u.shape[0]
    gw = w_grp.shape[2]
    tm = _row_tile(s_len, POOL_ROW_TILE)
    assert tm % POOL_CHUNK == 0
    halo_per_tile = tm // POOL_HALO
    n_halo = s_len // POOL_HALO
    return pl.pallas_call(
        functools.partial(_pool_group_kernel, seq_len=s_len, win=POOL_WINDOWS[grp]),
        grid=(s_len // tm,),
        in_specs=[
            pl.BlockSpec((POOL_HALO, gw), lambda i: (jnp.maximum(i * halo_per_tile - 1, 0), grp)),
            pl.BlockSpec((tm, gw), lambda i: (i, grp)),
            pl.BlockSpec((POOL_HALO, gw),
                         lambda i: (jnp.minimum((i + 1) * halo_per_tile, n_halo - 1), grp)),
            _resident((None, None, gw, gw), lambda i: (layer, grp, 0, 0)),
            pl.BlockSpec((1, gw), lambda i: (0, grp)),
            pl.BlockSpec((tm, gw), lambda i: (i, N_POOL_GROUPS + grp)),
        ],
        out_specs=pl.BlockSpec((tm, gw), lambda i: (i, 0)),
        out_shape=jax.ShapeDtypeStruct((s_len, gw), BF16),
        scratch_shapes=[pltpu.VMEM((tm, gw), BF16)],
        compiler_params=_params("arbitrary"),
        name=f"pool_group{grp}",
    )(u, u, u, w_grp, scale.reshape(1, -1), u)


def _attention_layer(x, xg, rinv, w_in, layer, q_g, k_g, sink, w_out, next_gain):
    g_cols = jnp.concatenate([jnp.tile(q_g.astype(F32), N_HEADS),
                              jnp.tile(k_g.astype(F32), N_KV),
                              jnp.ones((KV_W,), F32)]).reshape(1, QKV_W)
    qkv = _matmul_headnorm(xg, rinv, w_in, layer, g_cols, QKV_W, Q_W + KV_W)
    z = _matmul_normed(xg, rinv, w_in, layer, QKV_W, Q_W, F32, "matmul_attn_gate")
    gated = _attention(qkv, z, sink.astype(F32))
    return _matmul_residual([gated], w_out, layer, x, next_gain)


def _pool_layer(x, xg, rinv, w_in, layer, w_grp, scale, w_out, next_gain):
    u = _matmul_normed(xg, rinv, w_in, layer, 0, w_in.shape[2], F32, "matmul_pool_in")
    gated = [_pool_group(u, w_grp, layer, scale, grp) for grp in range(N_POOL_GROUPS)]
    return _matmul_residual(gated, w_out, layer, x, next_gain)


def kernel(x, p, norm_g, attn_w_in, attn_q_norm_g, attn_k_norm_g, attn_sink, attn_w_out,
           pool_w_in, pool_w_grp, pool_scale, pool_w_out, ple_norm_g, ple_w_gate, ple_w_proj):
    b, s_len, d = x.shape
    assert b == 1
    depth = p.shape[0]
    xf = x.reshape(s_len, d)
    xg, rinv = _prep(xf, norm_g[0])
    for i in range(depth):
        j = i // 2
        if i % 2 == 0:
            xf, xg, rinv = _attention_layer(xf, xg, rinv, attn_w_in, j, attn_q_norm_g[j],
                                            attn_k_norm_g[j], attn_sink[j], attn_w_out,
                                            ple_norm_g[i])
        else:
            xf, xg, rinv = _pool_layer(xf, xg, rinv, pool_w_in, j, pool_w_grp,
                                       pool_scale[j], pool_w_out, ple_norm_g[i])
        next_gain = norm_g[i + 1] if i + 1 < depth else None
        xf, xg, rinv = _per_layer_embed(xf, xg, rinv, p, ple_w_gate, ple_w_proj, i, next_gain)
    return xf.reshape(b, s_len, d)
```

```python
import functools
import math

import jax
import jax.numpy as jnp
from jax import lax
from jax.experimental import pallas as pl
from jax.experimental.pallas import tpu as pltpu

F32 = jnp.float32
BF16 = jnp.bfloat16

HEAD_DIM = 128
N_KV = 8
GROUP = 4
N_HEADS = N_KV * GROUP
Q_W = N_HEADS * HEAD_DIM
KV_W = N_KV * HEAD_DIM
QKV_W = Q_W + 2 * KV_W
WINDOW = 128
BLOCK = 128
ATTN_SCALE = HEAD_DIM ** -0.5
POOL_WINDOWS = (2, 4, 8, 16)
N_POOL_GROUPS = len(POOL_WINDOWS)
EPS = 1e-6
LOG2E = math.log2(math.e)
MASKED_SCORE = -1e30

SUBLANES = 8
LANES = 128
BF16_SUBLANE_PACK = 16
VMEM_LIMIT_BYTES = 58 * 1024 * 1024
POOL_HALO = SUBLANES

ROW_TILE = 1024
COL_TILE = 512
COL_TILE_PLAIN = 1024
POOL_ROW_TILE = 512
POOL_CHUNK = 128
PREP_ROW_TILE = 512


def _params(*semantics):
    return pltpu.CompilerParams(dimension_semantics=semantics,
                                vmem_limit_bytes=VMEM_LIMIT_BYTES)


def _row_tile(m, want):
    t = min(m, want)
    assert m % t == 0 and t % SUBLANES == 0
    return t


def _dot(a, b):
    return jnp.dot(a, b, preferred_element_type=F32)


def _resident(block_shape, index_map):
    return pl.BlockSpec(block_shape, index_map, pipeline_mode=pl.Buffered(1))


def _call(body, name, grid, semantics, in_specs, operands, out_specs, out_shapes,
          casts=(), scratch_shapes=()):
    n_steps = math.prod(grid)
    strides = [math.prod(grid[d + 1:]) for d in range(len(grid))]
    cast_in, cast_out, cast_shapes, cast_ops = [], [], [], []
    for w, lead in casts:
        rows, cols = w.shape[len(lead):]
        rb = BF16_SUBLANE_PACK
        while rb * n_steps < rows:
            rb *= 2
        assert rows % rb == 0
        last = rows // rb - 1

        def block_of(*idx, last=last):
            return jnp.minimum(sum(i * s for i, s in zip(idx, strides)), last)

        cast_in.append(pl.BlockSpec((None,) * len(lead) + (rb, cols),
                                    lambda *idx, lead=lead, b=block_of: lead + (b(*idx), 0)))
        cast_out.append(pl.BlockSpec((rb, cols), lambda *idx, b=block_of: (b(*idx), 0)))
        cast_shapes.append(jax.ShapeDtypeStruct((rows, cols), BF16))
        cast_ops.append(w)
    n_in, n_out, n_cast = len(in_specs), len(out_specs), len(cast_ops)

    def kernel(*refs):
        ins = refs[:n_in]
        srcs = refs[n_in:n_in + n_cast]
        outs = refs[n_in + n_cast:n_in + n_cast + n_out]
        dsts = refs[n_in + n_cast + n_out:n_in + 2 * n_cast + n_out]
        scratch = refs[n_in + 2 * n_cast + n_out:]
        body(*ins, *outs, *scratch)
        for src, dst in zip(srcs, dsts):
            dst[...] = src[...].astype(dst.dtype)

    results = pl.pallas_call(
        kernel,
        grid=grid,
        in_specs=list(in_specs) + cast_in,
        out_specs=list(out_specs) + cast_out,
        out_shape=list(out_shapes) + cast_shapes,
        scratch_shapes=list(scratch_shapes),
        compiler_params=_params(*semantics),
        name=name,
    )(*operands, *cast_ops)
    return results[:n_out], results[n_out:]


def _emit_norm_inputs(x_new, gain_ref, xg_ref, rinv_ref, d_model):
    j = pl.program_id(1)
    xg_ref[...] = (x_new * gain_ref[...]).astype(xg_ref.dtype)
    part = jnp.sum(x_new * x_new, axis=-1, keepdims=True)

    @pl.when(j == 0)
    def _():
        rinv_ref[...] = part

    @pl.when(j > 0)
    def _():
        rinv_ref[...] += part

    @pl.when(j == pl.num_programs(1) - 1)
    def _():
        rinv_ref[...] = lax.rsqrt(rinv_ref[...] * (1.0 / d_model) + EPS)


def _prep_kernel(x_ref, g_ref, xg_ref, rinv_ref):
    x = x_ref[...]
    xg_ref[...] = (x * g_ref[...]).astype(xg_ref.dtype)
    rinv_ref[...] = lax.rsqrt(jnp.mean(x * x, axis=-1, keepdims=True) + EPS)


def _prep(x, g, casts):
    m, d = x.shape
    tm = _row_tile(m, PREP_ROW_TILE)
    return _call(
        _prep_kernel, "norm_prep", (m // tm,), ("arbitrary",),
        [pl.BlockSpec((tm, d), lambda i: (i, 0)),
         pl.BlockSpec((1, d), lambda i: (0, 0))],
        [x, g.reshape(1, d)],
        [pl.BlockSpec((tm, d), lambda i: (i, 0)),
         pl.BlockSpec((tm, 1), lambda i: (i, 0))],
        [jax.ShapeDtypeStruct((m, d), BF16), jax.ShapeDtypeStruct((m, 1), F32)],
        casts)


def _normed_dot(a_ref, rinv_ref, w_ref):
    return _dot(a_ref[...], w_ref[...]) * rinv_ref[...]


def _mm_kernel(a_ref, rinv_ref, w_ref, o_ref):
    o_ref[...] = _normed_dot(a_ref, rinv_ref, w_ref).astype(o_ref.dtype)


def _matmul_normed(a, rinv, w, col0, n, out_dtype, name, casts):
    m, k = a.shape
    tm, tn = _row_tile(m, ROW_TILE), COL_TILE_PLAIN
    assert n % tn == 0 and col0 % tn == 0
    j0 = col0 // tn
    (out,), cast = _call(
        _mm_kernel, name, (m // tm, n // tn), ("arbitrary", "arbitrary"),
        [pl.BlockSpec((tm, k), lambda i, j: (i, 0)),
         pl.BlockSpec((tm, 1), lambda i, j: (i, 0)),
         pl.BlockSpec((k, tn), lambda i, j: (0, j0 + j))],
        [a, rinv, w],
        [pl.BlockSpec((tm, tn), lambda i, j: (i, j))],
        [jax.ShapeDtypeStruct((m, n), out_dtype)],
        casts)
    return out, cast


def _mm_headnorm_kernel(a_ref, rinv_ref, w_ref, g_ref, o_ref, *, norm_tiles):
    acc = _normed_dot(a_ref, rinv_ref, w_ref)
    j = pl.program_id(1)

    @pl.when(j < norm_tiles)
    def _():
        for h in range(acc.shape[1] // HEAD_DIM):
            sl = slice(h * HEAD_DIM, (h + 1) * HEAD_DIM)
            a = acc[:, sl]
            ms = jnp.mean(a * a, axis=-1, keepdims=True)
            o_ref[:, sl] = (a * lax.rsqrt(ms + EPS) * g_ref[:, sl]).astype(o_ref.dtype)

    @pl.when(j >= norm_tiles)
    def _():
        o_ref[...] = acc.astype(o_ref.dtype)


def _matmul_headnorm(a, rinv, w, g_cols, n, norm_cols, casts):
    m, k = a.shape
    tm, tn = _row_tile(m, ROW_TILE), COL_TILE
    assert n % tn == 0 and norm_cols % tn == 0
    (out,), cast = _call(
        functools.partial(_mm_headnorm_kernel, norm_tiles=norm_cols // tn),
        "matmul_headnorm", (m // tm, n // tn), ("arbitrary", "arbitrary"),
        [pl.BlockSpec((tm, k), lambda i, j: (i, 0)),
         pl.BlockSpec((tm, 1), lambda i, j: (i, 0)),
         pl.BlockSpec((k, tn), lambda i, j: (0, j)),
         pl.BlockSpec((1, tn), lambda i, j: (0, j))],
        [a, rinv, w, g_cols],
        [pl.BlockSpec((tm, tn), lambda i, j: (i, j))],
        [jax.ShapeDtypeStruct((m, n), BF16)],
        casts)
    return out, cast


def _mm_residual_kernel(*refs, n_a):
    a_refs = refs[:n_a]
    w_ref, res_ref, gain_ref, x_ref, xg_ref, rinv_ref = refs[n_a:]
    kc = a_refs[0].shape[1]
    acc = res_ref[...]
    for c, a_ref in enumerate(a_refs):
        acc = acc + _dot(a_ref[...], w_ref[c * kc:(c + 1) * kc, :])
    x_ref[...] = acc
    _emit_norm_inputs(acc, gain_ref, xg_ref, rinv_ref, x_ref.shape[1] * pl.num_programs(1))


def _matmul_residual(a_list, w, res, next_gain, casts):
    m, kc = a_list[0].shape
    k, n = w.shape
    n_a = len(a_list)
    assert kc * n_a == k
    tm, tn = _row_tile(m, ROW_TILE), COL_TILE
    a_spec = pl.BlockSpec((tm, kc), lambda i, j: (i, 0)) if k <= 4096 else \
        _resident((tm, kc), lambda i, j: (i, 0))
    tile = pl.BlockSpec((tm, tn), lambda i, j: (i, j))
    return _call(
        functools.partial(_mm_residual_kernel, n_a=n_a),
        "matmul_residual", (m // tm, n // tn), ("arbitrary", "arbitrary"),
        [a_spec] * n_a + [pl.BlockSpec((k, tn), lambda i, j: (0, j)),
                          tile,
                          pl.BlockSpec((1, tn), lambda i, j: (0, j))],
        [*a_list, w, res, next_gain.reshape(1, n)],
        [tile, tile, pl.BlockSpec((tm, 1), lambda i, j: (i, 0))],
        [jax.ShapeDtypeStruct((m, n), F32),
         jax.ShapeDtypeStruct((m, n), BF16),
         jax.ShapeDtypeStruct((m, 1), F32)],
        casts)


def _ple_update(a_ref, rinv_ref, wg_ref, p_ref, wp_ref, res_ref):
    gate = jax.nn.sigmoid(_normed_dot(a_ref, rinv_ref, wg_ref))
    proj = _dot(p_ref[...].astype(BF16), wp_ref[...])
    return res_ref[...] + proj * gate


def _ple_kernel(a_ref, rinv_ref, wg_ref, p_ref, wp_ref, res_ref, gain_ref,
                x_ref, xg_ref, rinv_out_ref):
    x_new = _ple_update(a_ref, rinv_ref, wg_ref, p_ref, wp_ref, res_ref)
    x_ref[...] = x_new
    _emit_norm_inputs(x_new, gain_ref, xg_ref, rinv_out_ref,
                      x_ref.shape[1] * pl.num_programs(1))


def _ple_last_kernel(a_ref, rinv_ref, wg_ref, p_ref, wp_ref, res_ref, x_ref):
    x_ref[...] = _ple_update(a_ref, rinv_ref, wg_ref, p_ref, wp_ref, res_ref)


def _per_layer_embed(x, xg, rinv, p, layer, w_gate, w_proj, next_gain, casts):
    m, k = xg.shape
    n = w_gate.shape[1]
    kp = p.shape[3]
    tm, tn = _row_tile(m, ROW_TILE), COL_TILE
    tile = pl.BlockSpec((tm, tn), lambda i, j: (i, j))
    col = pl.BlockSpec((tm, 1), lambda i, j: (i, 0))
    in_specs = [pl.BlockSpec((tm, k), lambda i, j: (i, 0)),
                col,
                pl.BlockSpec((k, tn), lambda i, j: (0, j)),
                _resident((None, None, tm, kp), lambda i, j: (layer, 0, i, 0)),
                pl.BlockSpec((kp, tn), lambda i, j: (0, j)),
                tile]
    operands = [xg, rinv, w_gate, p, w_proj, x]
    grid = (m // tm, n // tn)
    sem = ("arbitrary", "arbitrary")
    x_shape = jax.ShapeDtypeStruct((m, n), F32)
    if next_gain is None:
        return _call(_ple_last_kernel, "per_layer_embed_last", grid, sem, in_specs, operands,
                     [tile], [x_shape], casts)
    return _call(
        _ple_kernel, "per_layer_embed", grid, sem,
        in_specs + [pl.BlockSpec((1, tn), lambda i, j: (0, j))],
        operands + [next_gain.reshape(1, n)],
        [tile, tile, col],
        [x_shape, jax.ShapeDtypeStruct((m, n), BF16), jax.ShapeDtypeStruct((m, 1), F32)],
        casts)


def _attn_heads(sink_ref, bias_ref, q_ref, kp_ref, kc_ref, kn_ref, vp_ref, vc_ref, vn_ref,
                z_ref, o_ref, p_ref, *, outside):
    qscale = ATTN_SCALE * LOG2E
    for kv in range(N_KV):
        ksl = slice(kv * HEAD_DIM, (kv + 1) * HEAD_DIM)
        q4 = jnp.concatenate(
            [q_ref[:, (kv * GROUP + g) * HEAD_DIM:(kv * GROUP + g + 1) * HEAD_DIM]
             for g in range(GROUP)], axis=0)
        kw = jnp.concatenate([kp_ref[:, ksl], kc_ref[:, ksl], kn_ref[:, ksl]], axis=0)
        vw = jnp.concatenate([vp_ref[:, ksl], vc_ref[:, ksl], vn_ref[:, ksl]], axis=0)
        s = lax.dot_general(q4, kw, (((1,), (1,)), ((), ())), preferred_element_type=F32)
        for g in range(GROUP):
            head = kv * GROUP + g
            sg = s[g * BLOCK:(g + 1) * BLOCK] * qscale + bias_ref[head]
            if outside is not None:
                sg = jnp.where(outside, MASKED_SCORE, sg)
            sink2 = sink_ref[head] * LOG2E
            mx = jnp.maximum(jnp.max(sg, axis=-1, keepdims=True), sink2)
            e = jnp.exp2(sg - mx)
            denom = jnp.sum(e, axis=-1, keepdims=True) + jnp.exp2(sink2 - mx)
            p_ref[g * BLOCK:(g + 1) * BLOCK, :] = (e * (1.0 / denom)).astype(p_ref.dtype)
        o4 = _dot(p_ref[...], vw)
        for g in range(GROUP):
            sl = slice((kv * GROUP + g) * HEAD_DIM, (kv * GROUP + g + 1) * HEAD_DIM)
            z = z_ref[:, sl]
            o_ref[:, sl] = (o4[g * BLOCK:(g + 1) * BLOCK]
                            * (z * jax.nn.sigmoid(z))).astype(o_ref.dtype)


def _attn_kernel(*refs):
    bi = pl.program_id(0)
    is_first = bi == 0
    is_last = bi == pl.num_programs(0) - 1
    on_edge = is_first | is_last

    @pl.when(jnp.logical_not(on_edge))
    def _():
        _attn_heads(*refs, outside=None)

    @pl.when(on_edge)
    def _():
        col = lax.broadcasted_iota(jnp.int32, (BLOCK, 3 * BLOCK), 1)
        outside = (is_first & (col < BLOCK)) | (is_last & (col >= 2 * BLOCK))
        _attn_heads(*refs, outside=outside)


def _attention_bias():
    slopes = 2.0 ** (-8.0 * jnp.arange(1, N_HEADS + 1, dtype=F32) / N_HEADS)
    t = jnp.arange(BLOCK)[:, None]
    c = jnp.arange(3 * BLOCK)[None, :]
    dist = jnp.abs(t + BLOCK - c)
    bias = -(slopes * LOG2E)[:, None, None] * dist.astype(F32)[None]
    return jnp.where((dist <= WINDOW)[None], bias, MASKED_SCORE)


def _attention(qkv, z, sink):
    s_len = qkv.shape[0]
    nb = s_len // BLOCK
    k_blk = Q_W // KV_W
    v_blk = k_blk + 1

    def prev(bi):
        return jnp.maximum(bi - 1, 0)

    def nxt(bi):
        return jnp.minimum(bi + 1, nb - 1)

    kv_block = (BLOCK, KV_W)
    q_spec = pl.BlockSpec((BLOCK, Q_W), lambda bi: (bi, 0))
    return pl.pallas_call(
        _attn_kernel,
        grid=(nb,),
        in_specs=[pl.BlockSpec(memory_space=pltpu.SMEM),
                  _resident((N_HEADS, BLOCK, 3 * BLOCK), lambda bi: (0, 0, 0)),
                  q_spec,
                  pl.BlockSpec(kv_block, lambda bi: (prev(bi), k_blk)),
                  pl.BlockSpec(kv_block, lambda bi: (bi, k_blk)),
                  pl.BlockSpec(kv_block, lambda bi: (nxt(bi), k_blk)),
                  pl.BlockSpec(kv_block, lambda bi: (prev(bi), v_blk)),
                  pl.BlockSpec(kv_block, lambda bi: (bi, v_blk)),
                  pl.BlockSpec(kv_block, lambda bi: (nxt(bi), v_blk)),
                  q_spec],
        out_specs=q_spec,
        out_shape=jax.ShapeDtypeStruct((s_len, Q_W), BF16),
        scratch_shapes=[pltpu.VMEM((GROUP * BLOCK, 3 * BLOCK), BF16)],
        compiler_params=_params("arbitrary"),
        name="banded_attention",
    )(sink, _attention_bias(), qkv, qkv, qkv, qkv, qkv, qkv, qkv, z)


def _window_sum(e, win, rows):
    n = e.shape[0]
    h = POOL_HALO

    def ahead(x, k):
        return pltpu.roll(x, n - k, axis=0)

    def behind(x, k):
        return pltpu.roll(x, k, axis=0)

    if win == 2:
        return (e + ahead(e, 1))[h:h + rows]
    q1 = e + behind(e, 1)
    if win == 4:
        return (q1 + ahead(q1, 2))[h:h + rows]
    q2 = q1 + behind(q1, 2)
    if win == 8:
        return (q2 + ahead(q2, 4))[h:h + rows]
    assert win == 16
    q3 = q2 + behind(q2, 4)
    return q3[h:h + rows] + q3[2 * h:2 * h + rows]


def _pool_group_kernel(vp_ref, vc_ref, vn_ref, w_ref, scale_ref, z_ref, o_ref, d_ref,
                       *, seq_len, win):
    i = pl.program_id(0)
    n_i = pl.num_programs(0)
    tm = vc_ref.shape[0]
    left = (win - 1) // 2
    right = win - 1 - left
    h = POOL_HALO
    n_chunks = tm // POOL_CHUNK
    for c in range(n_chunks):
        r0 = c * POOL_CHUNK
        top = jnp.where(i > 0, vp_ref[...], 0.0) if c == 0 else vc_ref[r0 - h:r0, :]
        bot = (jnp.where(i < n_i - 1, vn_ref[...], 0.0) if c == n_chunks - 1
               else vc_ref[r0 + POOL_CHUNK:r0 + POOL_CHUNK + h, :])
        cur = vc_ref[r0:r0 + POOL_CHUNK, :]
        e = jnp.concatenate([top, cur, bot], axis=0)
        t = i * tm + r0 + lax.broadcasted_iota(jnp.int32, (POOL_CHUNK, 1), 0)
        lo = jnp.maximum(t - left, 0)
        hi = jnp.minimum(t + right + 1, seq_len)
        inv_cnt = 1.0 / (hi - lo).astype(F32)
        d_ref[r0:r0 + POOL_CHUNK, :] = (
            _window_sum(e, win, POOL_CHUNK) * inv_cnt - cur).astype(d_ref.dtype)
    y = _dot(d_ref[...], w_ref[...].astype(BF16)) * scale_ref[...]
    z = z_ref[...]
    o_ref[...] = (y * (z * jax.nn.sigmoid(z))).astype(o_ref.dtype)


def _pool_group(u, w_grp, layer, scale, grp):
    s_len = u.shape[0]
    gw = w_grp.shape[2]
    tm = _row_tile(s_len, POOL_ROW_TILE)
    assert tm % POOL_CHUNK == 0
    halo_per_tile = tm // POOL_HALO
    n_halo = s_len // POOL_HALO
    return pl.pallas_call(
        functools.partial(_pool_group_kernel, seq_len=s_len, win=POOL_WINDOWS[grp]),
        grid=(s_len // tm,),
        in_specs=[
            pl.BlockSpec((POOL_HALO, gw), lambda i: (jnp.maximum(i * halo_per_tile - 1, 0), grp)),
            pl.BlockSpec((tm, gw), lambda i: (i, grp)),
            pl.BlockSpec((POOL_HALO, gw),
                         lambda i: (jnp.minimum((i + 1) * halo_per_tile, n_halo - 1), grp)),
            _resident((None, None, gw, gw), lambda i: (layer, grp, 0, 0)),
            pl.BlockSpec((1, gw), lambda i: (0, grp)),
            pl.BlockSpec((tm, gw), lambda i: (i, N_POOL_GROUPS + grp)),
        ],
        out_specs=pl.BlockSpec((tm, gw), lambda i: (i, 0)),
        out_shape=jax.ShapeDtypeStruct((s_len, gw), BF16),
        scratch_shapes=[pltpu.VMEM((tm, gw), BF16)],
        compiler_params=_params("arbitrary"),
        name=f"pool_group{grp}",
    )(u, u, u, w_grp, scale.reshape(1, -1), u)


def kernel(x, p, norm_g, attn_w_in, attn_q_norm_g, attn_k_norm_g, attn_sink, attn_w_out,
           pool_w_in, pool_w_grp, pool_scale, pool_w_out, ple_norm_g, ple_w_gate, ple_w_proj):
    b, s_len, d = x.shape
    assert b == 1
    depth = p.shape[0]
    assert depth % 2 == 0
    xf = x.reshape(s_len, d)
    (xg, rinv), (w_in,) = _prep(xf, norm_g[0], [(attn_w_in, (0,))])
    for pair in range(depth // 2):
        i = 2 * pair
        g_cols = jnp.concatenate([jnp.tile(attn_q_norm_g[pair].astype(F32), N_HEADS),
                                  jnp.tile(attn_k_norm_g[pair].astype(F32), N_KV),
                                  jnp.ones((KV_W,), F32)]).reshape(1, QKV_W)
        qkv, (w_out,) = _matmul_headnorm(xg, rinv, w_in, g_cols, QKV_W, Q_W + KV_W,
                                         [(attn_w_out, (pair,))])
        z, (w_gate, w_proj) = _matmul_normed(xg, rinv, w_in, QKV_W, Q_W, F32, "matmul_attn_gate",
                                             [(ple_w_gate, (i,)), (ple_w_proj, (i,))])
        gated = _attention(qkv, z, attn_sink[pair].astype(F32))
        (xf, xg, rinv), (w_in,) = _matmul_residual([gated], w_out, xf, ple_norm_g[i],
                                                   [(pool_w_in, (pair,))])
        (xf, xg, rinv), (w_out,) = _per_layer_embed(xf, xg, rinv, p, i, w_gate, w_proj,
                                                    norm_g[i + 1], [(pool_w_out, (pair,))])
        u, (w_gate, w_proj) = _matmul_normed(xg, rinv, w_in, 0, w_in.shape[1], F32,
                                             "matmul_pool_in",
                                             [(ple_w_gate, (i + 1,)), (ple_w_proj, (i + 1,))])
        gated = [_pool_group(u, pool_w_grp, pair, pool_scale[pair], grp)
                 for grp in range(N_POOL_GROUPS)]
        last = i + 2 == depth
        (xf, xg, rinv), w_next = _matmul_residual(
            gated, w_out, xf, ple_norm_g[i + 1], [] if last else [(attn_w_in, (pair + 1,))])
        outs, _ = _per_layer_embed(xf, xg, rinv, p, i + 1, w_gate, w_proj,
                                   None if last else norm_g[i + 2], [])
        if last:
            (xf,) = outs
        else:
            xf, xg, rinv = outs
            (w_in,) = w_next
    return xf.reshape(b, s_len, d)
```

```python
import functools
import math

import jax
import jax.numpy as jnp
from jax import lax
from jax.experimental import pallas as pl
from jax.experimental.pallas import tpu as pltpu

F32 = jnp.float32
BF16 = jnp.bfloat16

HEAD_DIM = 128
N_KV = 8
GROUP = 4
N_HEADS = N_KV * GROUP
Q_W = N_HEADS * HEAD_DIM
KV_W = N_KV * HEAD_DIM
QKV_W = Q_W + 2 * KV_W
WINDOW = 128
BLOCK = 128
ATTN_SCALE = HEAD_DIM ** -0.5
POOL_WINDOWS = (2, 4, 8, 16)
N_POOL_GROUPS = len(POOL_WINDOWS)
EPS = 1e-6
LOG2E = math.log2(math.e)
MASKED_SCORE = -1e30

SUBLANES = 8
LANES = 128
BF16_SUBLANE_PACK = 16
VMEM_LIMIT_BYTES = 58 * 1024 * 1024
POOL_HALO = SUBLANES

ROW_TILE = 1024
COL_TILE = 512
COL_TILE_PLAIN = 1024
EPILOGUE_CHUNK = 256
POOL_ROW_TILE = 512
POOL_CHUNK = 128
PREP_ROW_TILE = 512


def _params(*semantics):
    return pltpu.CompilerParams(dimension_semantics=semantics,
                                vmem_limit_bytes=VMEM_LIMIT_BYTES)


def _row_tile(m, want):
    t = min(m, want)
    assert m % t == 0 and t % SUBLANES == 0
    return t


def _dot(a, b):
    return jnp.dot(a, b, preferred_element_type=F32)


def _sigmoid(x):
    return 0.5 * jnp.tanh(0.5 * x) + 0.5


def _silu(x):
    return x * _sigmoid(x)


def _resident(block_shape, index_map):
    return pl.BlockSpec(block_shape, index_map, pipeline_mode=pl.Buffered(1))


def _call(body, name, grid, semantics, in_specs, operands, out_specs, out_shapes,
          casts=(), scratch_shapes=()):
    n_steps = math.prod(grid)
    strides = [math.prod(grid[d + 1:]) for d in range(len(grid))]
    cast_in, cast_out, cast_shapes, cast_ops = [], [], [], []
    for w, lead in casts:
        rows, cols = w.shape[len(lead):]
        rb = BF16_SUBLANE_PACK
        while rb * n_steps < rows:
            rb *= 2
        assert rows % rb == 0
        last = rows // rb - 1

        def block_of(*idx, last=last):
            return jnp.minimum(sum(i * s for i, s in zip(idx, strides)), last)

        cast_in.append(pl.BlockSpec((None,) * len(lead) + (rb, cols),
                                    lambda *idx, lead=lead, b=block_of: lead + (b(*idx), 0)))
        cast_out.append(pl.BlockSpec((rb, cols), lambda *idx, b=block_of: (b(*idx), 0)))
        cast_shapes.append(jax.ShapeDtypeStruct((rows, cols), BF16))
        cast_ops.append(w)
    n_in, n_out, n_cast = len(in_specs), len(out_specs), len(cast_ops)

    def kernel(*refs):
        ins = refs[:n_in]
        srcs = refs[n_in:n_in + n_cast]
        outs = refs[n_in + n_cast:n_in + n_cast + n_out]
        dsts = refs[n_in + n_cast + n_out:n_in + 2 * n_cast + n_out]
        scratch = refs[n_in + 2 * n_cast + n_out:]
        body(*ins, *outs, *scratch)
        for src, dst in zip(srcs, dsts):
            dst[...] = src[...].astype(dst.dtype)

    results = pl.pallas_call(
        kernel,
        grid=grid,
        in_specs=list(in_specs) + cast_in,
        out_specs=list(out_specs) + cast_out,
        out_shape=list(out_shapes) + cast_shapes,
        scratch_shapes=list(scratch_shapes),
        compiler_params=_params(*semantics),
        name=name,
    )(*operands, *cast_ops)
    return results[:n_out], results[n_out:]


def _col_chunks(width):
    step = min(width, EPILOGUE_CHUNK)
    return [slice(c, c + step) for c in range(0, width, step)]


def _emit_norm_inputs(x_chunks, gain_ref, xg_ref, rinv_ref, d_model):
    j = pl.program_id(1)
    part = None
    for cols, x_new in x_chunks:
        xg_ref[:, cols] = (x_new * gain_ref[:, cols]).astype(xg_ref.dtype)
        ss = jnp.sum(x_new * x_new, axis=-1, keepdims=True)
        part = ss if part is None else part + ss

    @pl.when(j == 0)
    def _():
        rinv_ref[...] = part

    @pl.when(j > 0)
    def _():
        rinv_ref[...] += part

    @pl.when(j == pl.num_programs(1) - 1)
    def _():
        rinv_ref[...] = lax.rsqrt(rinv_ref[...] * (1.0 / d_model) + EPS)


def _prep_kernel(x_ref, g_ref, xg_ref, rinv_ref):
    x = x_ref[...]
    xg_ref[...] = (x * g_ref[...]).astype(xg_ref.dtype)
    rinv_ref[...] = lax.rsqrt(jnp.mean(x * x, axis=-1, keepdims=True) + EPS)


def _prep(x, g, casts):
    m, d = x.shape
    tm = _row_tile(m, PREP_ROW_TILE)
    return _call(
        _prep_kernel, "norm_prep", (m // tm,), ("arbitrary",),
        [pl.BlockSpec((tm, d), lambda i: (i, 0)),
         pl.BlockSpec((1, d), lambda i: (0, 0))],
        [x, g.reshape(1, d)],
        [pl.BlockSpec((tm, d), lambda i: (i, 0)),
         pl.BlockSpec((tm, 1), lambda i: (i, 0))],
        [jax.ShapeDtypeStruct((m, d), BF16), jax.ShapeDtypeStruct((m, 1), F32)],
        casts)


def _normed_dot(a_ref, rinv_ref, w_ref):
    return _dot(a_ref[...], w_ref[...]) * rinv_ref[...]


def _mm_kernel(a_ref, rinv_ref, w_ref, o_ref):
    o_ref[...] = _normed_dot(a_ref, rinv_ref, w_ref).astype(o_ref.dtype)


def _matmul_normed(a, rinv, w, col0, n, out_dtype, name, casts):
    m, k = a.shape
    tm, tn = _row_tile(m, ROW_TILE), COL_TILE_PLAIN
    assert n % tn == 0 and col0 % tn == 0
    j0 = col0 // tn
    (out,), cast = _call(
        _mm_kernel, name, (m // tm, n // tn), ("arbitrary", "arbitrary"),
        [pl.BlockSpec((tm, k), lambda i, j: (i, 0)),
         pl.BlockSpec((tm, 1), lambda i, j: (i, 0)),
         pl.BlockSpec((k, tn), lambda i, j: (0, j0 + j))],
        [a, rinv, w],
        [pl.BlockSpec((tm, tn), lambda i, j: (i, j))],
        [jax.ShapeDtypeStruct((m, n), out_dtype)],
        casts)
    return out, cast


def _mm_headnorm_kernel(a_ref, rinv_ref, w_ref, g_ref, o_ref, *, norm_tiles):
    j = pl.program_id(1)

    @pl.when(j < norm_tiles)
    def _():
        for cols in _col_chunks(o_ref.shape[1]):
            acc = _dot(a_ref[...], w_ref[:, cols]) * rinv_ref[...]
            for h in range(acc.shape[1] // HEAD_DIM):
                a = acc[:, h * HEAD_DIM:(h + 1) * HEAD_DIM]
                sl = slice(cols.start + h * HEAD_DIM, cols.start + (h + 1) * HEAD_DIM)
                ms = jnp.mean(a * a, axis=-1, keepdims=True)
                o_ref[:, sl] = (a * lax.rsqrt(ms + EPS) * g_ref[:, sl]).astype(o_ref.dtype)

    @pl.when(j >= norm_tiles)
    def _():
        o_ref[...] = _normed_dot(a_ref, rinv_ref, w_ref).astype(o_ref.dtype)


def _matmul_headnorm(a, rinv, w, g_cols, n, norm_cols, casts):
    m, k = a.shape
    tm, tn = _row_tile(m, ROW_TILE), COL_TILE
    assert n % tn == 0 and norm_cols % tn == 0
    (out,), cast = _call(
        functools.partial(_mm_headnorm_kernel, norm_tiles=norm_cols // tn),
        "matmul_headnorm", (m // tm, n // tn), ("arbitrary", "arbitrary"),
        [pl.BlockSpec((tm, k), lambda i, j: (i, 0)),
         pl.BlockSpec((tm, 1), lambda i, j: (i, 0)),
         pl.BlockSpec((k, tn), lambda i, j: (0, j)),
         pl.BlockSpec((1, tn), lambda i, j: (0, j))],
        [a, rinv, w, g_cols],
        [pl.BlockSpec((tm, tn), lambda i, j: (i, j))],
        [jax.ShapeDtypeStruct((m, n), BF16)],
        casts)
    return out, cast


def _mm_residual_kernel(*refs, n_a):
    a_refs = refs[:n_a]
    w_ref, res_ref, gain_ref, x_ref, xg_ref, rinv_ref = refs[n_a:]
    kc = a_refs[0].shape[1]
    x_chunks = []
    for cols in _col_chunks(x_ref.shape[1]):
        acc = res_ref[:, cols]
        for c, a_ref in enumerate(a_refs):
            acc = acc + _dot(a_ref[...], w_ref[c * kc:(c + 1) * kc, cols])
        x_ref[:, cols] = acc
        x_chunks.append((cols, acc))
    _emit_norm_inputs(x_chunks, gain_ref, xg_ref, rinv_ref,
                      x_ref.shape[1] * pl.num_programs(1))


def _matmul_residual(a_list, w, res, next_gain, casts):
    m, kc = a_list[0].shape
    k, n = w.shape
    n_a = len(a_list)
    assert kc * n_a == k
    tm, tn = _row_tile(m, ROW_TILE), COL_TILE
    a_spec = pl.BlockSpec((tm, kc), lambda i, j: (i, 0)) if k <= 4096 else \
        _resident((tm, kc), lambda i, j: (i, 0))
    tile = pl.BlockSpec((tm, tn), lambda i, j: (i, j))
    return _call(
        functools.partial(_mm_residual_kernel, n_a=n_a),
        "matmul_residual", (m // tm, n // tn), ("arbitrary", "arbitrary"),
        [a_spec] * n_a + [pl.BlockSpec((k, tn), lambda i, j: (0, j)),
                          tile,
                          pl.BlockSpec((1, tn), lambda i, j: (0, j))],
        [*a_list, w, res, next_gain.reshape(1, n)],
        [tile, tile, pl.BlockSpec((tm, 1), lambda i, j: (i, 0))],
        [jax.ShapeDtypeStruct((m, n), F32),
         jax.ShapeDtypeStruct((m, n), BF16),
         jax.ShapeDtypeStruct((m, 1), F32)],
        casts)


def _ple_update(a_ref, rinv_ref, wg_ref, p_ref, wp_ref, res_ref, x_ref):
    pb = p_ref[...].astype(BF16)
    x_chunks = []
    for cols in _col_chunks(x_ref.shape[1]):
        gate = _sigmoid(_dot(a_ref[...], wg_ref[:, cols]) * rinv_ref[...])
        x_new = res_ref[:, cols] + _dot(pb, wp_ref[:, cols]) * gate
        x_ref[:, cols] = x_new
        x_chunks.append((cols, x_new))
    return x_chunks


def _ple_kernel(a_ref, rinv_ref, wg_ref, p_ref, wp_ref, res_ref, gain_ref,
                x_ref, xg_ref, rinv_out_ref):
    x_chunks = _ple_update(a_ref, rinv_ref, wg_ref, p_ref, wp_ref, res_ref, x_ref)
    _emit_norm_inputs(x_chunks, gain_ref, xg_ref, rinv_out_ref,
                      x_ref.shape[1] * pl.num_programs(1))


def _ple_last_kernel(a_ref, rinv_ref, wg_ref, p_ref, wp_ref, res_ref, x_ref):
    _ple_update(a_ref, rinv_ref, wg_ref, p_ref, wp_ref, res_ref, x_ref)


def _per_layer_embed(x, xg, rinv, p, layer, w_gate, w_proj, next_gain, casts):
    m, k = xg.shape
    n = w_gate.shape[1]
    kp = p.shape[3]
    tm, tn = _row_tile(m, ROW_TILE), COL_TILE
    tile = pl.BlockSpec((tm, tn), lambda i, j: (i, j))
    col = pl.BlockSpec((tm, 1), lambda i, j: (i, 0))
    in_specs = [pl.BlockSpec((tm, k), lambda i, j: (i, 0)),
                col,
                pl.BlockSpec((k, tn), lambda i, j: (0, j)),
                _resident((None, None, tm, kp), lambda i, j: (layer, 0, i, 0)),
                pl.BlockSpec((kp, tn), lambda i, j: (0, j)),
                tile]
    operands = [xg, rinv, w_gate, p, w_proj, x]
    grid = (m // tm, n // tn)
    sem = ("arbitrary", "arbitrary")
    x_shape = jax.ShapeDtypeStruct((m, n), F32)
    if next_gain is None:
        return _call(_ple_last_kernel, "per_layer_embed_last", grid, sem, in_specs, operands,
                     [tile], [x_shape], casts)
    return _call(
        _ple_kernel, "per_layer_embed", grid, sem,
        in_specs + [pl.BlockSpec((1, tn), lambda i, j: (0, j))],
        operands + [next_gain.reshape(1, n)],
        [tile, tile, col],
        [x_shape, jax.ShapeDtypeStruct((m, n), BF16), jax.ShapeDtypeStruct((m, 1), F32)],
        casts)


def _attn_heads(sink_ref, bias_ref, q_ref, kvp_ref, kvc_ref, kvn_ref, z_ref, o_ref, p_ref,
                *, outside):
    qscale = ATTN_SCALE * LOG2E
    kv_refs = (kvp_ref, kvc_ref, kvn_ref)
    for kv in range(N_KV):
        ksl = slice(kv * HEAD_DIM, (kv + 1) * HEAD_DIM)
        vsl = slice(KV_W + kv * HEAD_DIM, KV_W + (kv + 1) * HEAD_DIM)
        q4 = jnp.concatenate(
            [q_ref[:, (kv * GROUP + g) * HEAD_DIM:(kv * GROUP + g + 1) * HEAD_DIM]
             for g in range(GROUP)], axis=0)
        kw = jnp.concatenate([r[:, ksl] for r in kv_refs], axis=0)
        vw = jnp.concatenate([r[:, vsl] for r in kv_refs], axis=0)
        s = lax.dot_general(q4, kw, (((1,), (1,)), ((), ())), preferred_element_type=F32)
        for g in range(GROUP):
            head = kv * GROUP + g
            sg = s[g * BLOCK:(g + 1) * BLOCK] * qscale + bias_ref[head]
            if outside is not None:
                sg = jnp.where(outside, MASKED_SCORE, sg)
            sink2 = sink_ref[head] * LOG2E
            mx = jnp.maximum(jnp.max(sg, axis=-1, keepdims=True), sink2)
            e = jnp.exp2(sg - mx)
            denom = jnp.sum(e, axis=-1, keepdims=True) + jnp.exp2(sink2 - mx)
            p_ref[g * BLOCK:(g + 1) * BLOCK, :] = (e * (1.0 / denom)).astype(p_ref.dtype)
        o4 = _dot(p_ref[...], vw)
        for g in range(GROUP):
            sl = slice((kv * GROUP + g) * HEAD_DIM, (kv * GROUP + g + 1) * HEAD_DIM)
            z = z_ref[:, sl]
            o_ref[:, sl] = (o4[g * BLOCK:(g + 1) * BLOCK] * _silu(z)).astype(o_ref.dtype)


def _attn_kernel(*refs):
    bi = pl.program_id(0)
    is_first = bi == 0
    is_last = bi == pl.num_programs(0) - 1
    on_edge = is_first | is_last

    @pl.when(jnp.logical_not(on_edge))
    def _():
        _attn_heads(*refs, outside=None)

    @pl.when(on_edge)
    def _():
        col = lax.broadcasted_iota(jnp.int32, (BLOCK, 3 * BLOCK), 1)
        outside = (is_first & (col < BLOCK)) | (is_last & (col >= 2 * BLOCK))
        _attn_heads(*refs, outside=outside)


def _attention_bias():
    slopes = 2.0 ** (-8.0 * jnp.arange(1, N_HEADS + 1, dtype=F32) / N_HEADS)
    t = jnp.arange(BLOCK)[:, None]
    c = jnp.arange(3 * BLOCK)[None, :]
    dist = jnp.abs(t + BLOCK - c)
    bias = -(slopes * LOG2E)[:, None, None] * dist.astype(F32)[None]
    return jnp.where((dist <= WINDOW)[None], bias, MASKED_SCORE)


def _attention(qkv, z, sink, casts):
    s_len = qkv.shape[0]
    nb = s_len // BLOCK
    kv_blk = Q_W // (2 * KV_W)
    assert kv_blk * 2 * KV_W == Q_W
    kv_block = (BLOCK, 2 * KV_W)
    q_spec = pl.BlockSpec((BLOCK, Q_W), lambda bi: (bi, 0))
    (out,), cast = _call(
        _attn_kernel, "banded_attention", (nb,), ("arbitrary",),
        [pl.BlockSpec(memory_space=pltpu.SMEM),
         _resident((N_HEADS, BLOCK, 3 * BLOCK), lambda bi: (0, 0, 0)),
         q_spec,
         pl.BlockSpec(kv_block, lambda bi: (jnp.maximum(bi - 1, 0), kv_blk)),
         pl.BlockSpec(kv_block, lambda bi: (bi, kv_blk)),
         pl.BlockSpec(kv_block, lambda bi: (jnp.minimum(bi + 1, nb - 1), kv_blk)),
         q_spec],
        [sink, _attention_bias(), qkv, qkv, qkv, qkv, z],
        [q_spec],
        [jax.ShapeDtypeStruct((s_len, Q_W), BF16)],
        casts,
        [pltpu.VMEM((GROUP * BLOCK, 3 * BLOCK), BF16)])
    return out, cast


def _window_sum(e, win, rows):
    n = e.shape[0]
    h = POOL_HALO

    def ahead(x, k):
        return pltpu.roll(x, n - k, axis=0)

    def behind(x, k):
        return pltpu.roll(x, k, axis=0)

    if win == 2:
        return (e + ahead(e, 1))[h:h + rows]
    q1 = e + behind(e, 1)
    if win == 4:
        return (q1 + ahead(q1, 2))[h:h + rows]
    q2 = q1 + behind(q1, 2)
    if win == 8:
        return (q2 + ahead(q2, 4))[h:h + rows]
    assert win == 16
    q3 = q2 + behind(q2, 4)
    return q3[h:h + rows] + q3[2 * h:2 * h + rows]


def _pool_group_kernel(vp_ref, vc_ref, vn_ref, w_ref, scale_ref, z_ref, o_ref, d_ref,
                       *, seq_len, win):
    i = pl.program_id(0)
    n_i = pl.num_programs(0)
    tm = vc_ref.shape[0]
    left = (win - 1) // 2
    right = win - 1 - left
    h = POOL_HALO
    n_chunks = tm // POOL_CHUNK
    for c in range(n_chunks):
        r0 = c * POOL_CHUNK
        top = jnp.where(i > 0, vp_ref[...], 0.0) if c == 0 else vc_ref[r0 - h:r0, :]
        bot = (jnp.where(i < n_i - 1, vn_ref[...], 0.0) if c == n_chunks - 1
               else vc_ref[r0 + POOL_CHUNK:r0 + POOL_CHUNK + h, :])
        cur = vc_ref[r0:r0 + POOL_CHUNK, :]
        e = jnp.concatenate([top, cur, bot], axis=0)
        t = i * tm + r0 + lax.broadcasted_iota(jnp.int32, (POOL_CHUNK, 1), 0)
        lo = jnp.maximum(t - left, 0)
        hi = jnp.minimum(t + right + 1, seq_len)
        inv_cnt = 1.0 / (hi - lo).astype(F32)
        d_ref[r0:r0 + POOL_CHUNK, :] = (
            _window_sum(e, win, POOL_CHUNK) * inv_cnt - cur).astype(d_ref.dtype)
    y = _dot(d_ref[...], w_ref[...].astype(BF16)) * scale_ref[...]
    o_ref[...] = (y * _silu(z_ref[...])).astype(o_ref.dtype)


def _pool_group(u, w_grp, layer, scale, grp):
    s_len = u.shape[0]
    gw = w_grp.shape[2]
    tm = _row_tile(s_len, POOL_ROW_TILE)
    assert tm % POOL_CHUNK == 0
    halo_per_tile = tm // POOL_HALO
    n_halo = s_len // POOL_HALO
    return pl.pallas_call(
        functools.partial(_pool_group_kernel, seq_len=s_len, win=POOL_WINDOWS[grp]),
        grid=(s_len // tm,),
        in_specs=[
            pl.BlockSpec((POOL_HALO, gw), lambda i: (jnp.maximum(i * halo_per_tile - 1, 0), grp)),
            pl.BlockSpec((tm, gw), lambda i: (i, grp)),
            pl.BlockSpec((POOL_HALO, gw),
                         lambda i: (jnp.minimum((i + 1) * halo_per_tile, n_halo - 1), grp)),
            _resident((None, None, gw, gw), lambda i: (layer, grp, 0, 0)),
            pl.BlockSpec((1, gw), lambda i: (0, grp)),
            pl.BlockSpec((tm, gw), lambda i: (i, N_POOL_GROUPS + grp)),
        ],
        out_specs=pl.BlockSpec((tm, gw), lambda i: (i, 0)),
        out_shape=jax.ShapeDtypeStruct((s_len, gw), BF16),
        scratch_shapes=[pltpu.VMEM((tm, gw), BF16)],
        compiler_params=_params("arbitrary"),
        name=f"pool_group{grp}",
    )(u, u, u, w_grp, scale.reshape(1, -1), u)


def kernel(x, p, norm_g, attn_w_in, attn_q_norm_g, attn_k_norm_g, attn_sink, attn_w_out,
           pool_w_in, pool_w_grp, pool_scale, pool_w_out, ple_norm_g, ple_w_gate, ple_w_proj):
    b, s_len, d = x.shape
    assert b == 1
    depth = p.shape[0]
    assert depth % 2 == 0
    xf = x.reshape(s_len, d)
    (xg, rinv), (w_in,) = _prep(xf, norm_g[0], [(attn_w_in, (0,))])
    for pair in range(depth // 2):
        i = 2 * pair
        g_cols = jnp.concatenate([jnp.tile(attn_q_norm_g[pair].astype(F32), N_HEADS),
                                  jnp.tile(attn_k_norm_g[pair].astype(F32), N_KV),
                                  jnp.ones((KV_W,), F32)]).reshape(1, QKV_W)
        qkv, (w_out, w_pool_in) = _matmul_headnorm(
            xg, rinv, w_in, g_cols, QKV_W, Q_W + KV_W,
            [(attn_w_out, (pair,)), (pool_w_in, (pair,))])
        z, _ = _matmul_normed(xg, rinv, w_in, QKV_W, Q_W, F32, "matmul_attn_gate", [])
        gated, (w_gate, w_proj, w_pool_out) = _attention(
            qkv, z, attn_sink[pair].astype(F32),
            [(ple_w_gate, (i,)), (ple_w_proj, (i,)), (pool_w_out, (pair,))])
        (xf, xg, rinv), _ = _matmul_residual([gated], w_out, xf, ple_norm_g[i], [])
        (xf, xg, rinv), _ = _per_layer_embed(xf, xg, rinv, p, i, w_gate, w_proj,
                                             norm_g[i + 1], [])
        last = i + 2 == depth
        u, (w_gate, w_proj, *w_next) = _matmul_normed(
            xg, rinv, w_pool_in, 0, w_pool_in.shape[1], F32, "matmul_pool_in",
            [(ple_w_gate, (i + 1,)), (ple_w_proj, (i + 1,))]
            + ([] if last else [(attn_w_in, (pair + 1,))]))
        gated = [_pool_group(u, pool_w_grp, pair, pool_scale[pair], grp)
                 for grp in range(N_POOL_GROUPS)]
        (xf, xg, rinv), _ = _matmul_residual(gated, w_pool_out, xf, ple_norm_g[i + 1], [])
        outs, _ = _per_layer_embed(xf, xg, rinv, p, i + 1, w_gate, w_proj,
                                   None if last else norm_g[i + 2], [])
        if last:
            (xf,) = outs
        else:
            xf, xg, rinv = outs
            (w_in,) = w_next
    return xf.reshape(b, s_len, d)
```

```python
import functools
import math

import jax
import jax.numpy as jnp
from jax import lax
from jax.experimental import pallas as pl
from jax.experimental.pallas import tpu as pltpu

F32 = jnp.float32
BF16 = jnp.bfloat16

HEAD_DIM = 128
N_KV = 8
GROUP = 4
N_HEADS = N_KV * GROUP
Q_W = N_HEADS * HEAD_DIM
KV_W = N_KV * HEAD_DIM
QKV_W = Q_W + 2 * KV_W
WINDOW = 128
BLOCK = 128
ATTN_SCALE = HEAD_DIM ** -0.5
POOL_WINDOWS = (2, 4, 8, 16)
N_POOL_GROUPS = len(POOL_WINDOWS)
EPS = 1e-6
LOG2E = math.log2(math.e)
MASKED_SCORE = -1e30

SUBLANES = 8
LANES = 128
BF16_SUBLANE_PACK = 16
VMEM_LIMIT_BYTES = 58 * 1024 * 1024
POOL_HALO = SUBLANES

ROW_TILE = 1024
COL_TILE = 512
COL_TILE_WIDE_K = 256
COL_TILE_PLAIN = 1024
EPILOGUE_CHUNK = 256
POOL_ROW_TILE = 512
POOL_CHUNK = 128
PREP_ROW_TILE = 512


def _params(*semantics):
    return pltpu.CompilerParams(dimension_semantics=semantics,
                                vmem_limit_bytes=VMEM_LIMIT_BYTES)


def _row_tile(m, want):
    t = min(m, want)
    assert m % t == 0 and t % SUBLANES == 0
    return t


def _dot(a, b):
    return jnp.dot(a, b, preferred_element_type=F32)


def _sigmoid(x):
    return 0.5 * jnp.tanh(0.5 * x) + 0.5


def _silu(x):
    return x * _sigmoid(x)


def _resident(block_shape, index_map):
    return pl.BlockSpec(block_shape, index_map, pipeline_mode=pl.Buffered(1))


def _call(body, name, grid, semantics, in_specs, operands, out_specs, out_shapes,
          casts=(), scratch_shapes=()):
    n_steps = math.prod(grid)
    strides = [math.prod(grid[d + 1:]) for d in range(len(grid))]
    cast_in, cast_out, cast_shapes, cast_ops = [], [], [], []
    for w, lead in casts:
        rows, cols = w.shape[len(lead):]
        rb = BF16_SUBLANE_PACK
        while rb * n_steps < rows:
            rb *= 2
        assert rows % rb == 0
        last = rows // rb - 1

        def block_of(*idx, last=last):
            return jnp.minimum(sum(i * s for i, s in zip(idx, strides)), last)

        cast_in.append(pl.BlockSpec((None,) * len(lead) + (rb, cols),
                                    lambda *idx, lead=lead, b=block_of: lead + (b(*idx), 0)))
        cast_out.append(pl.BlockSpec((rb, cols), lambda *idx, b=block_of: (b(*idx), 0)))
        cast_shapes.append(jax.ShapeDtypeStruct((rows, cols), BF16))
        cast_ops.append(w)
    n_in, n_out, n_cast = len(in_specs), len(out_specs), len(cast_ops)

    def kernel(*refs):
        ins = refs[:n_in]
        srcs = refs[n_in:n_in + n_cast]
        outs = refs[n_in + n_cast:n_in + n_cast + n_out]
        dsts = refs[n_in + n_cast + n_out:n_in + 2 * n_cast + n_out]
        scratch = refs[n_in + 2 * n_cast + n_out:]
        body(*ins, *outs, *scratch)
        for src, dst in zip(srcs, dsts):
            dst[...] = src[...].astype(dst.dtype)

    results = pl.pallas_call(
        kernel,
        grid=grid,
        in_specs=list(in_specs) + cast_in,
        out_specs=list(out_specs) + cast_out,
        out_shape=list(out_shapes) + cast_shapes,
        scratch_shapes=list(scratch_shapes),
        compiler_params=_params(*semantics),
        name=name,
    )(*operands, *cast_ops)
    return results[:n_out], results[n_out:]


def _col_chunks(width):
    step = min(width, EPILOGUE_CHUNK)
    return [slice(c, c + step) for c in range(0, width, step)]


def _emit_norm_inputs(x_chunks, gain_ref, xg_ref, rinv_ref, d_model):
    j = pl.program_id(1)
    part = None
    for cols, x_new in x_chunks:
        xg_ref[:, cols] = (x_new * gain_ref[:, cols]).astype(xg_ref.dtype)
        ss = jnp.sum(x_new * x_new, axis=-1, keepdims=True)
        part = ss if part is None else part + ss

    @pl.when(j == 0)
    def _():
        rinv_ref[...] = part

    @pl.when(j > 0)
    def _():
        rinv_ref[...] += part

    @pl.when(j == pl.num_programs(1) - 1)
    def _():
        rinv_ref[...] = lax.rsqrt(rinv_ref[...] * (1.0 / d_model) + EPS)


def _prep_kernel(x_ref, g_ref, xg_ref, rinv_ref):
    x = x_ref[...]
    xg_ref[...] = (x * g_ref[...]).astype(xg_ref.dtype)
    rinv_ref[...] = lax.rsqrt(jnp.mean(x * x, axis=-1, keepdims=True) + EPS)


def _prep(x, g, casts):
    m, d = x.shape
    tm = _row_tile(m, PREP_ROW_TILE)
    return _call(
        _prep_kernel, "norm_prep", (m // tm,), ("arbitrary",),
        [pl.BlockSpec((tm, d), lambda i: (i, 0)),
         pl.BlockSpec((1, d), lambda i: (0, 0))],
        [x, g.reshape(1, d)],
        [pl.BlockSpec((tm, d), lambda i: (i, 0)),
         pl.BlockSpec((tm, 1), lambda i: (i, 0))],
        [jax.ShapeDtypeStruct((m, d), BF16), jax.ShapeDtypeStruct((m, 1), F32)],
        casts)


def _normed_dot(a_ref, rinv_ref, w_ref):
    return _dot(a_ref[...], w_ref[...]) * rinv_ref[...]


def _mm_kernel(a_ref, rinv_ref, w_ref, o_ref):
    o_ref[...] = _normed_dot(a_ref, rinv_ref, w_ref).astype(o_ref.dtype)


def _matmul_normed(a, rinv, w, col0, n, out_dtype, name, casts):
    m, k = a.shape
    tm, tn = _row_tile(m, ROW_TILE), COL_TILE_PLAIN
    assert n % tn == 0 and col0 % tn == 0
    j0 = col0 // tn
    (out,), cast = _call(
        _mm_kernel, name, (m // tm, n // tn), ("arbitrary", "arbitrary"),
        [pl.BlockSpec((tm, k), lambda i, j: (i, 0)),
         pl.BlockSpec((tm, 1), lambda i, j: (i, 0)),
         pl.BlockSpec((k, tn), lambda i, j: (0, j0 + j))],
        [a, rinv, w],
        [pl.BlockSpec((tm, tn), lambda i, j: (i, j))],
        [jax.ShapeDtypeStruct((m, n), out_dtype)],
        casts)
    return out, cast


def _mm_headnorm_kernel(a_ref, rinv_ref, w_ref, g_ref, o_ref, *, norm_tiles):
    j = pl.program_id(1)

    @pl.when(j < norm_tiles)
    def _():
        for cols in _col_chunks(o_ref.shape[1]):
            acc = _dot(a_ref[...], w_ref[:, cols]) * rinv_ref[...]
            for h in range(acc.shape[1] // HEAD_DIM):
                a = acc[:, h * HEAD_DIM:(h + 1) * HEAD_DIM]
                sl = slice(cols.start + h * HEAD_DIM, cols.start + (h + 1) * HEAD_DIM)
                ms = jnp.mean(a * a, axis=-1, keepdims=True)
                o_ref[:, sl] = (a * lax.rsqrt(ms + EPS) * g_ref[:, sl]).astype(o_ref.dtype)

    @pl.when(j >= norm_tiles)
    def _():
        o_ref[...] = _normed_dot(a_ref, rinv_ref, w_ref).astype(o_ref.dtype)


def _matmul_headnorm(a, rinv, w, g_cols, n, norm_cols, casts):
    m, k = a.shape
    tm, tn = _row_tile(m, ROW_TILE), COL_TILE
    assert n % tn == 0 and norm_cols % tn == 0
    (out,), cast = _call(
        functools.partial(_mm_headnorm_kernel, norm_tiles=norm_cols // tn),
        "matmul_headnorm", (m // tm, n // tn), ("arbitrary", "arbitrary"),
        [pl.BlockSpec((tm, k), lambda i, j: (i, 0)),
         pl.BlockSpec((tm, 1), lambda i, j: (i, 0)),
         pl.BlockSpec((k, tn), lambda i, j: (0, j)),
         pl.BlockSpec((1, tn), lambda i, j: (0, j))],
        [a, rinv, w, g_cols],
        [pl.BlockSpec((tm, tn), lambda i, j: (i, j))],
        [jax.ShapeDtypeStruct((m, n), BF16)],
        casts)
    return out, cast


def _mm_residual_kernel(*refs, n_a):
    a_refs = refs[:n_a]
    w_ref, res_ref, gain_ref, x_ref, xg_ref, rinv_ref = refs[n_a:]
    kc = a_refs[0].shape[1]
    x_chunks = []
    for cols in _col_chunks(x_ref.shape[1]):
        acc = res_ref[:, cols]
        for c, a_ref in enumerate(a_refs):
            acc = acc + _dot(a_ref[...], w_ref[c * kc:(c + 1) * kc, cols])
        x_ref[:, cols] = acc
        x_chunks.append((cols, acc))
    _emit_norm_inputs(x_chunks, gain_ref, xg_ref, rinv_ref,
                      x_ref.shape[1] * pl.num_programs(1))


def _matmul_residual(a_list, w, res, next_gain, casts):
    m, kc = a_list[0].shape
    k, n = w.shape
    n_a = len(a_list)
    assert kc * n_a == k
    tm, tn = _row_tile(m, ROW_TILE), (COL_TILE if k <= 4096 else COL_TILE_WIDE_K)
    a_spec = pl.BlockSpec((tm, kc), lambda i, j: (i, 0))
    tile = pl.BlockSpec((tm, tn), lambda i, j: (i, j))
    return _call(
        functools.partial(_mm_residual_kernel, n_a=n_a),
        "matmul_residual", (m // tm, n // tn), ("arbitrary", "arbitrary"),
        [a_spec] * n_a + [pl.BlockSpec((k, tn), lambda i, j: (0, j)),
                          tile,
                          pl.BlockSpec((1, tn), lambda i, j: (0, j))],
        [*a_list, w, res, next_gain.reshape(1, n)],
        [tile, tile, pl.BlockSpec((tm, 1), lambda i, j: (i, 0))],
        [jax.ShapeDtypeStruct((m, n), F32),
         jax.ShapeDtypeStruct((m, n), BF16),
         jax.ShapeDtypeStruct((m, 1), F32)],
        casts)


def _ple_update(a_ref, rinv_ref, wg_ref, p_ref, wp_ref, res_ref, x_ref):
    pb = p_ref[...].astype(BF16)
    x_chunks = []
    for cols in _col_chunks(x_ref.shape[1]):
        gate = _sigmoid(_dot(a_ref[...], wg_ref[:, cols]) * rinv_ref[...])
        x_new = res_ref[:, cols] + _dot(pb, wp_ref[:, cols]) * gate
        x_ref[:, cols] = x_new
        x_chunks.append((cols, x_new))
    return x_chunks


def _ple_kernel(a_ref, rinv_ref, wg_ref, p_ref, wp_ref, res_ref, gain_ref,
                x_ref, xg_ref, rinv_out_ref):
    x_chunks = _ple_update(a_ref, rinv_ref, wg_ref, p_ref, wp_ref, res_ref, x_ref)
    _emit_norm_inputs(x_chunks, gain_ref, xg_ref, rinv_out_ref,
                      x_ref.shape[1] * pl.num_programs(1))


def _ple_last_kernel(a_ref, rinv_ref, wg_ref, p_ref, wp_ref, res_ref, x_ref):
    _ple_update(a_ref, rinv_ref, wg_ref, p_ref, wp_ref, res_ref, x_ref)


def _per_layer_embed(x, xg, rinv, p, layer, w_gate, w_proj, next_gain, casts):
    m, k = xg.shape
    n = w_gate.shape[1]
    kp = p.shape[3]
    tm, tn = _row_tile(m, ROW_TILE), COL_TILE
    tile = pl.BlockSpec((tm, tn), lambda i, j: (i, j))
    col = pl.BlockSpec((tm, 1), lambda i, j: (i, 0))
    in_specs = [pl.BlockSpec((tm, k), lambda i, j: (i, 0)),
                col,
                pl.BlockSpec((k, tn), lambda i, j: (0, j)),
                _resident((None, None, tm, kp), lambda i, j: (layer, 0, i, 0)),
                pl.BlockSpec((kp, tn), lambda i, j: (0, j)),
                tile]
    operands = [xg, rinv, w_gate, p, w_proj, x]
    grid = (m // tm, n // tn)
    sem = ("arbitrary", "arbitrary")
    x_shape = jax.ShapeDtypeStruct((m, n), F32)
    if next_gain is None:
        return _call(_ple_last_kernel, "per_layer_embed_last", grid, sem, in_specs, operands,
                     [tile], [x_shape], casts)
    return _call(
        _ple_kernel, "per_layer_embed", grid, sem,
        in_specs + [pl.BlockSpec((1, tn), lambda i, j: (0, j))],
        operands + [next_gain.reshape(1, n)],
        [tile, tile, col],
        [x_shape, jax.ShapeDtypeStruct((m, n), BF16), jax.ShapeDtypeStruct((m, 1), F32)],
        casts)


def _attn_heads(sink_ref, bias_ref, q_ref, kvp_ref, kvc_ref, kvn_ref, z_ref, o_ref, p_ref,
                *, outside):
    qscale = ATTN_SCALE * LOG2E
    kv_refs = (kvp_ref, kvc_ref, kvn_ref)
    for kv in range(N_KV):
        ksl = slice(kv * HEAD_DIM, (kv + 1) * HEAD_DIM)
        vsl = slice(KV_W + kv * HEAD_DIM, KV_W + (kv + 1) * HEAD_DIM)
        q4 = jnp.concatenate(
            [q_ref[:, (kv * GROUP + g) * HEAD_DIM:(kv * GROUP + g + 1) * HEAD_DIM]
             for g in range(GROUP)], axis=0)
        kw = jnp.concatenate([r[:, ksl] for r in kv_refs], axis=0)
        vw = jnp.concatenate([r[:, vsl] for r in kv_refs], axis=0)
        s = lax.dot_general(q4, kw, (((1,), (1,)), ((), ())), preferred_element_type=F32)
        inv = []
        for g in range(GROUP):
            head = kv * GROUP + g
            sg = s[g * BLOCK:(g + 1) * BLOCK] * qscale + bias_ref[head]
            if outside is not None:
                sg = jnp.where(outside, MASKED_SCORE, sg)
            sink2 = sink_ref[head] * LOG2E
            mx = jnp.maximum(jnp.max(sg, axis=-1, keepdims=True), sink2)
            e = jnp.exp2(sg - mx)
            denom = jnp.sum(e, axis=-1, keepdims=True) + jnp.exp2(sink2 - mx)
            inv.append(1.0 / denom)
            p_ref[g * BLOCK:(g + 1) * BLOCK, :] = e.astype(p_ref.dtype)
        o4 = _dot(p_ref[...], vw)
        for g in range(GROUP):
            sl = slice((kv * GROUP + g) * HEAD_DIM, (kv * GROUP + g + 1) * HEAD_DIM)
            z = z_ref[:, sl]
            o_ref[:, sl] = (o4[g * BLOCK:(g + 1) * BLOCK] * (inv[g] * _silu(z))
                            ).astype(o_ref.dtype)


def _attn_kernel(*refs):
    bi = pl.program_id(0)
    is_first = bi == 0
    is_last = bi == pl.num_programs(0) - 1
    on_edge = is_first | is_last

    @pl.when(jnp.logical_not(on_edge))
    def _():
        _attn_heads(*refs, outside=None)

    @pl.when(on_edge)
    def _():
        col = lax.broadcasted_iota(jnp.int32, (BLOCK, 3 * BLOCK), 1)
        outside = (is_first & (col < BLOCK)) | (is_last & (col >= 2 * BLOCK))
        _attn_heads(*refs, outside=outside)


def _attention_bias():
    slopes = 2.0 ** (-8.0 * jnp.arange(1, N_HEADS + 1, dtype=F32) / N_HEADS)
    t = jnp.arange(BLOCK)[:, None]
    c = jnp.arange(3 * BLOCK)[None, :]
    dist = jnp.abs(t + BLOCK - c)
    bias = -(slopes * LOG2E)[:, None, None] * dist.astype(F32)[None]
    return jnp.where((dist <= WINDOW)[None], bias, MASKED_SCORE)


def _attention(qkv, z, sink, casts):
    s_len = qkv.shape[0]
    nb = s_len // BLOCK
    kv_blk = Q_W // (2 * KV_W)
    assert kv_blk * 2 * KV_W == Q_W
    kv_block = (BLOCK, 2 * KV_W)
    q_spec = pl.BlockSpec((BLOCK, Q_W), lambda bi: (bi, 0))
    (out,), cast = _call(
        _attn_kernel, "banded_attention", (nb,), ("arbitrary",),
        [pl.BlockSpec(memory_space=pltpu.SMEM),
         _resident((N_HEADS, BLOCK, 3 * BLOCK), lambda bi: (0, 0, 0)),
         q_spec,
         pl.BlockSpec(kv_block, lambda bi: (jnp.maximum(bi - 1, 0), kv_blk)),
         pl.BlockSpec(kv_block, lambda bi: (bi, kv_blk)),
         pl.BlockSpec(kv_block, lambda bi: (jnp.minimum(bi + 1, nb - 1), kv_blk)),
         q_spec],
        [sink, _attention_bias(), qkv, qkv, qkv, qkv, z],
        [q_spec],
        [jax.ShapeDtypeStruct((s_len, Q_W), BF16)],
        casts,
        [pltpu.VMEM((GROUP * BLOCK, 3 * BLOCK), BF16)])
    return out, cast


def _window_sum(e, win, rows):
    n = e.shape[0]
    h = POOL_HALO

    def ahead(x, k):
        return pltpu.roll(x, n - k, axis=0)

    def behind(x, k):
        return pltpu.roll(x, k, axis=0)

    if win == 2:
        return (e + ahead(e, 1))[h:h + rows]
    q1 = e + behind(e, 1)
    if win == 4:
        return (q1 + ahead(q1, 2))[h:h + rows]
    q2 = q1 + behind(q1, 2)
    if win == 8:
        return (q2 + ahead(q2, 4))[h:h + rows]
    assert win == 16
    q3 = q2 + behind(q2, 4)
    return q3[h:h + rows] + q3[2 * h:2 * h + rows]


def _pool_group_kernel(vp_ref, vc_ref, vn_ref, w_ref, scale_ref, z_ref, o_ref, d_ref,
                       *, seq_len, win):
    i = pl.program_id(0)
    n_i = pl.num_programs(0)
    tm = vc_ref.shape[0]
    left = (win - 1) // 2
    right = win - 1 - left
    h = POOL_HALO
    n_chunks = tm // POOL_CHUNK
    for c in range(n_chunks):
        r0 = c * POOL_CHUNK
        top = jnp.where(i > 0, vp_ref[...], 0.0) if c == 0 else vc_ref[r0 - h:r0, :]
        bot = (jnp.where(i < n_i - 1, vn_ref[...], 0.0) if c == n_chunks - 1
               else vc_ref[r0 + POOL_CHUNK:r0 + POOL_CHUNK + h, :])
        cur = vc_ref[r0:r0 + POOL_CHUNK, :]
        e = jnp.concatenate([top, cur, bot], axis=0)
        t = i * tm + r0 + lax.broadcasted_iota(jnp.int32, (POOL_CHUNK, 1), 0)
        lo = jnp.maximum(t - left, 0)
        hi = jnp.minimum(t + right + 1, seq_len)
        inv_cnt = 1.0 / (hi - lo).astype(F32)
        d_ref[r0:r0 + POOL_CHUNK, :] = (
            _window_sum(e, win, POOL_CHUNK) * inv_cnt - cur).astype(d_ref.dtype)
    y = _dot(d_ref[...], w_ref[...].astype(BF16)) * scale_ref[...]
    o_ref[...] = (y * _silu(z_ref[...])).astype(o_ref.dtype)


def _pool_group(u, w_grp, layer, scale, grp):
    s_len = u.shape[0]
    gw = w_grp.shape[2]
    tm = _row_tile(s_len, POOL_ROW_TILE)
    assert tm % POOL_CHUNK == 0
    halo_per_tile = tm // POOL_HALO
    n_halo = s_len // POOL_HALO
    return pl.pallas_call(
        functools.partial(_pool_group_kernel, seq_len=s_len, win=POOL_WINDOWS[grp]),
        grid=(s_len // tm,),
        in_specs=[
            pl.BlockSpec((POOL_HALO, gw), lambda i: (jnp.maximum(i * halo_per_tile - 1, 0), grp)),
            pl.BlockSpec((tm, gw), lambda i: (i, grp)),
            pl.BlockSpec((POOL_HALO, gw),
                         lambda i: (jnp.minimum((i + 1) * halo_per_tile, n_halo - 1), grp)),
            _resident((None, None, gw, gw), lambda i: (layer, grp, 0, 0)),
            pl.BlockSpec((1, gw), lambda i: (0, grp)),
            pl.BlockSpec((tm, gw), lambda i: (i, N_POOL_GROUPS + grp)),
        ],
        out_specs=pl.BlockSpec((tm, gw), lambda i: (i, 0)),
        out_shape=jax.ShapeDtypeStruct((s_len, gw), BF16),
        scratch_shapes=[pltpu.VMEM((tm, gw), BF16)],
        compiler_params=_params("arbitrary"),
        name=f"pool_group{grp}",
    )(u, u, u, w_grp, scale.reshape(1, -1), u)


def kernel(x, p, norm_g, attn_w_in, attn_q_norm_g, attn_k_norm_g, attn_sink, attn_w_out,
           pool_w_in, pool_w_grp, pool_scale, pool_w_out, ple_norm_g, ple_w_gate, ple_w_proj):
    b, s_len, d = x.shape
    assert b == 1
    depth = p.shape[0]
    assert depth % 2 == 0
    xf = x.reshape(s_len, d)
    (xg, rinv), (w_in,) = _prep(xf, norm_g[0], [(attn_w_in, (0,))])
    for pair in range(depth // 2):
        i = 2 * pair
        g_cols = jnp.concatenate([jnp.tile(attn_q_norm_g[pair].astype(F32), N_HEADS),
                                  jnp.tile(attn_k_norm_g[pair].astype(F32), N_KV),
                                  jnp.ones((KV_W,), F32)]).reshape(1, QKV_W)
        qkv, (w_out, w_pool_in) = _matmul_headnorm(
            xg, rinv, w_in, g_cols, QKV_W, Q_W + KV_W,
            [(attn_w_out, (pair,)), (pool_w_in, (pair,))])
        z, _ = _matmul_normed(xg, rinv, w_in, QKV_W, Q_W, F32, "matmul_attn_gate", [])
        gated, (w_gate, w_proj, w_pool_out) = _attention(
            qkv, z, attn_sink[pair].astype(F32),
            [(ple_w_gate, (i,)), (ple_w_proj, (i,)), (pool_w_out, (pair,))])
        (xf, xg, rinv), _ = _matmul_residual([gated], w_out, xf, ple_norm_g[i], [])
        (xf, xg, rinv), _ = _per_layer_embed(xf, xg, rinv, p, i, w_gate, w_proj,
                                             norm_g[i + 1], [])
        last = i + 2 == depth
        u, (w_gate, w_proj, *w_next) = _matmul_normed(
            xg, rinv, w_pool_in, 0, w_pool_in.shape[1], F32, "matmul_pool_in",
            [(ple_w_gate, (i + 1,)), (ple_w_proj, (i + 1,))]
            + ([] if last else [(attn_w_in, (pair + 1,))]))
        gated = [_pool_group(u, pool_w_grp, pair, pool_scale[pair], grp)
                 for grp in range(N_POOL_GROUPS)]
        (xf, xg, rinv), _ = _matmul_residual(gated, w_pool_out, xf, ple_norm_g[i + 1], [])
        outs, _ = _per_layer_embed(xf, xg, rinv, p, i + 1, w_gate, w_proj,
                                   None if last else norm_g[i + 2], [])
        if last:
            (xf,) = outs
        else:
            xf, xg, rinv = outs
            (w_in,) = w_next
    return xf.reshape(b, s_len, d)
```

```python
import functools
import math

import jax
import jax.numpy as jnp
from jax import lax
from jax.experimental import pallas as pl
from jax.experimental.pallas import tpu as pltpu

F32 = jnp.float32
BF16 = jnp.bfloat16

HEAD_DIM = 128
N_KV = 8
GROUP = 4
N_HEADS = N_KV * GROUP
Q_W = N_HEADS * HEAD_DIM
KV_W = N_KV * HEAD_DIM
QKV_W = Q_W + 2 * KV_W
WINDOW = 128
BLOCK = 128
ATTN_SCALE = HEAD_DIM ** -0.5
POOL_WINDOWS = (2, 4, 8, 16)
N_POOL_GROUPS = len(POOL_WINDOWS)
EPS = 1e-6
LOG2E = math.log2(math.e)
MASKED_SCORE = -1e30

SUBLANES = 8
LANES = 128
BF16_SUBLANE_PACK = 16
VMEM_LIMIT_BYTES = 58 * 1024 * 1024
POOL_HALO = SUBLANES

ROW_TILE = 1024
COL_TILE = 512
COL_TILE_WIDE_K = 256
COL_TILE_PLAIN = 1024
EPILOGUE_CHUNK = 256
POOL_ROW_TILE = 512
POOL_CHUNK = 128
PREP_ROW_TILE = 512


def _params(*semantics):
    return pltpu.CompilerParams(dimension_semantics=semantics,
                                vmem_limit_bytes=VMEM_LIMIT_BYTES)


def _row_tile(m, want):
    t = min(m, want)
    assert m % t == 0 and t % SUBLANES == 0
    return t


def _dot(a, b):
    return jnp.dot(a, b, preferred_element_type=F32)


def _sigmoid(x):
    return 0.5 * jnp.tanh(0.5 * x) + 0.5


def _silu(x):
    return x * _sigmoid(x)


def _resident(block_shape, index_map):
    return pl.BlockSpec(block_shape, index_map, pipeline_mode=pl.Buffered(1))


def _call(body, name, grid, semantics, in_specs, operands, out_specs, out_shapes,
          casts=(), scratch_shapes=()):
    n_steps = math.prod(grid)
    strides = [math.prod(grid[d + 1:]) for d in range(len(grid))]
    cast_in, cast_out, cast_shapes, cast_ops = [], [], [], []
    for w, lead in casts:
        rows, cols = w.shape[len(lead):]
        rb = BF16_SUBLANE_PACK
        while rb * n_steps < rows:
            rb *= 2
        assert rows % rb == 0
        last = rows // rb - 1

        def block_of(*idx, last=last):
            return jnp.minimum(sum(i * s for i, s in zip(idx, strides)), last)

        cast_in.append(pl.BlockSpec((None,) * len(lead) + (rb, cols),
                                    lambda *idx, lead=lead, b=block_of: lead + (b(*idx), 0)))
        cast_out.append(pl.BlockSpec((rb, cols), lambda *idx, b=block_of: (b(*idx), 0)))
        cast_shapes.append(jax.ShapeDtypeStruct((rows, cols), BF16))
        cast_ops.append(w)
    n_in, n_out, n_cast = len(in_specs), len(out_specs), len(cast_ops)

    def kernel(*refs):
        ins = refs[:n_in]
        srcs = refs[n_in:n_in + n_cast]
        outs = refs[n_in + n_cast:n_in + n_cast + n_out]
        dsts = refs[n_in + n_cast + n_out:n_in + 2 * n_cast + n_out]
        scratch = refs[n_in + 2 * n_cast + n_out:]
        body(*ins, *outs, *scratch)
        for src, dst in zip(srcs, dsts):
            dst[...] = src[...].astype(dst.dtype)

    results = pl.pallas_call(
        kernel,
        grid=grid,
        in_specs=list(in_specs) + cast_in,
        out_specs=list(out_specs) + cast_out,
        out_shape=list(out_shapes) + cast_shapes,
        scratch_shapes=list(scratch_shapes),
        compiler_params=_params(*semantics),
        name=name,
    )(*operands, *cast_ops)
    return results[:n_out], results[n_out:]


def _col_chunks(width):
    step = min(width, EPILOGUE_CHUNK)
    return [slice(c, c + step) for c in range(0, width, step)]


def _emit_norm_inputs(x_chunks, gain_ref, xg_ref, rinv_ref, d_model):
    j = pl.program_id(1)
    part = None
    for cols, x_new in x_chunks:
        xg_ref[:, cols] = (x_new * gain_ref[:, cols]).astype(xg_ref.dtype)
        ss = jnp.sum(x_new * x_new, axis=-1, keepdims=True)
        part = ss if part is None else part + ss

    @pl.when(j == 0)
    def _():
        rinv_ref[...] = part

    @pl.when(j > 0)
    def _():
        rinv_ref[...] += part

    @pl.when(j == pl.num_programs(1) - 1)
    def _():
        rinv_ref[...] = lax.rsqrt(rinv_ref[...] * (1.0 / d_model) + EPS)


def _prep_kernel(x_ref, g_ref, xg_ref, rinv_ref):
    x = x_ref[...]
    xg_ref[...] = (x * g_ref[...]).astype(xg_ref.dtype)
    rinv_ref[...] = lax.rsqrt(jnp.mean(x * x, axis=-1, keepdims=True) + EPS)


def _prep(x, g, casts):
    m, d = x.shape
    tm = _row_tile(m, PREP_ROW_TILE)
    return _call(
        _prep_kernel, "norm_prep", (m // tm,), ("arbitrary",),
        [pl.BlockSpec((tm, d), lambda i: (i, 0)),
         pl.BlockSpec((1, d), lambda i: (0, 0))],
        [x, g.reshape(1, d)],
        [pl.BlockSpec((tm, d), lambda i: (i, 0)),
         pl.BlockSpec((tm, 1), lambda i: (i, 0))],
        [jax.ShapeDtypeStruct((m, d), BF16), jax.ShapeDtypeStruct((m, 1), F32)],
        casts)


def _normed_dot(a_ref, rinv_ref, w_ref):
    return _dot(a_ref[...], w_ref[...]) * rinv_ref[...]


def _mm_kernel(a_ref, rinv_ref, w_ref, o_ref):
    o_ref[...] = _normed_dot(a_ref, rinv_ref, w_ref).astype(o_ref.dtype)


def _matmul_normed(a, rinv, w, col0, n, out_dtype, name, casts):
    m, k = a.shape
    tm, tn = _row_tile(m, ROW_TILE), COL_TILE_PLAIN
    assert n % tn == 0 and col0 % tn == 0
    j0 = col0 // tn
    (out,), cast = _call(
        _mm_kernel, name, (m // tm, n // tn), ("arbitrary", "arbitrary"),
        [pl.BlockSpec((tm, k), lambda i, j: (i, 0)),
         pl.BlockSpec((tm, 1), lambda i, j: (i, 0)),
         pl.BlockSpec((k, tn), lambda i, j: (0, j0 + j))],
        [a, rinv, w],
        [pl.BlockSpec((tm, tn), lambda i, j: (i, j))],
        [jax.ShapeDtypeStruct((m, n), out_dtype)],
        casts)
    return out, cast


def _mm_headnorm_kernel(a_ref, rinv_ref, w_ref, g_ref, o_ref, *, norm_tiles):
    j = pl.program_id(1)

    @pl.when(j < norm_tiles)
    def _():
        for cols in _col_chunks(o_ref.shape[1]):
            acc = _dot(a_ref[...], w_ref[:, cols]) * rinv_ref[...]
            for h in range(acc.shape[1] // HEAD_DIM):
                a = acc[:, h * HEAD_DIM:(h + 1) * HEAD_DIM]
                sl = slice(cols.start + h * HEAD_DIM, cols.start + (h + 1) * HEAD_DIM)
                ms = jnp.mean(a * a, axis=-1, keepdims=True)
                o_ref[:, sl] = (a * lax.rsqrt(ms + EPS) * g_ref[:, sl]).astype(o_ref.dtype)

    @pl.when(j >= norm_tiles)
    def _():
        o_ref[...] = _normed_dot(a_ref, rinv_ref, w_ref).astype(o_ref.dtype)


def _matmul_headnorm(a, rinv, w, g_cols, n, norm_cols, casts):
    m, k = a.shape
    tm, tn = _row_tile(m, ROW_TILE), COL_TILE
    assert n % tn == 0 and norm_cols % tn == 0
    (out,), cast = _call(
        functools.partial(_mm_headnorm_kernel, norm_tiles=norm_cols // tn),
        "matmul_headnorm", (m // tm, n // tn), ("arbitrary", "arbitrary"),
        [pl.BlockSpec((tm, k), lambda i, j: (i, 0)),
         pl.BlockSpec((tm, 1), lambda i, j: (i, 0)),
         pl.BlockSpec((k, tn), lambda i, j: (0, j)),
         pl.BlockSpec((1, tn), lambda i, j: (0, j))],
        [a, rinv, w, g_cols],
        [pl.BlockSpec((tm, tn), lambda i, j: (i, j))],
        [jax.ShapeDtypeStruct((m, n), BF16)],
        casts)
    return out, cast


def _mm_residual_kernel(*refs, n_a):
    a_refs = refs[:n_a]
    w_ref, res_ref, gain_ref, x_ref, xg_ref, rinv_ref = refs[n_a:]
    kc = a_refs[0].shape[1]
    x_chunks = []
    for cols in _col_chunks(x_ref.shape[1]):
        acc = res_ref[:, cols]
        for c, a_ref in enumerate(a_refs):
            acc = acc + _dot(a_ref[...], w_ref[c * kc:(c + 1) * kc, cols])
        x_ref[:, cols] = acc
        x_chunks.append((cols, acc))
    _emit_norm_inputs(x_chunks, gain_ref, xg_ref, rinv_ref,
                      x_ref.shape[1] * pl.num_programs(1))


def _matmul_residual(a_list, w, res, next_gain, casts):
    m, kc = a_list[0].shape
    k, n = w.shape
    n_a = len(a_list)
    assert kc * n_a == k
    tm, tn = _row_tile(m, ROW_TILE), (COL_TILE if k <= 4096 else COL_TILE_WIDE_K)
    a_spec = pl.BlockSpec((tm, kc), lambda i, j: (i, 0))
    tile = pl.BlockSpec((tm, tn), lambda i, j: (i, j))
    return _call(
        functools.partial(_mm_residual_kernel, n_a=n_a),
        "matmul_residual", (m // tm, n // tn), ("arbitrary", "arbitrary"),
        [a_spec] * n_a + [pl.BlockSpec((k, tn), lambda i, j: (0, j)),
                          tile,
                          pl.BlockSpec((1, tn), lambda i, j: (0, j))],
        [*a_list, w, res, next_gain.reshape(1, n)],
        [tile, tile, pl.BlockSpec((tm, 1), lambda i, j: (i, 0))],
        [jax.ShapeDtypeStruct((m, n), F32),
         jax.ShapeDtypeStruct((m, n), BF16),
         jax.ShapeDtypeStruct((m, 1), F32)],
        casts)


def _ple_update(a_ref, rinv_ref, wg_ref, p_ref, wp_ref, res_ref, x_ref):
    pb = p_ref[...].astype(BF16)
    x_chunks = []
    for cols in _col_chunks(x_ref.shape[1]):
        gate = _sigmoid(_dot(a_ref[...], wg_ref[:, cols]) * rinv_ref[...])
        x_new = res_ref[:, cols] + _dot(pb, wp_ref[:, cols]) * gate
        x_ref[:, cols] = x_new
        x_chunks.append((cols, x_new))
    return x_chunks


def _ple_kernel(a_ref, rinv_ref, wg_ref, p_ref, wp_ref, res_ref, gain_ref,
                x_ref, xg_ref, rinv_out_ref):
    x_chunks = _ple_update(a_ref, rinv_ref, wg_ref, p_ref, wp_ref, res_ref, x_ref)
    _emit_norm_inputs(x_chunks, gain_ref, xg_ref, rinv_out_ref,
                      x_ref.shape[1] * pl.num_programs(1))


def _ple_last_kernel(a_ref, rinv_ref, wg_ref, p_ref, wp_ref, res_ref, x_ref):
    _ple_update(a_ref, rinv_ref, wg_ref, p_ref, wp_ref, res_ref, x_ref)


def _per_layer_embed(x, xg, rinv, p, layer, w_gate, w_proj, next_gain, casts):
    m, k = xg.shape
    n = w_gate.shape[1]
    kp = p.shape[3]
    tm, tn = _row_tile(m, ROW_TILE), COL_TILE
    tile = pl.BlockSpec((tm, tn), lambda i, j: (i, j))
    col = pl.BlockSpec((tm, 1), lambda i, j: (i, 0))
    in_specs = [pl.BlockSpec((tm, k), lambda i, j: (i, 0)),
                col,
                pl.BlockSpec((k, tn), lambda i, j: (0, j)),
                _resident((None, None, tm, kp), lambda i, j: (layer, 0, i, 0)),
                pl.BlockSpec((kp, tn), lambda i, j: (0, j)),
                tile]
    operands = [xg, rinv, w_gate, p, w_proj, x]
    grid = (m // tm, n // tn)
    sem = ("arbitrary", "arbitrary")
    x_shape = jax.ShapeDtypeStruct((m, n), F32)
    if next_gain is None:
        return _call(_ple_last_kernel, "per_layer_embed_last", grid, sem, in_specs, operands,
                     [tile], [x_shape], casts)
    return _call(
        _ple_kernel, "per_layer_embed", grid, sem,
        in_specs + [pl.BlockSpec((1, tn), lambda i, j: (0, j))],
        operands + [next_gain.reshape(1, n)],
        [tile, tile, col],
        [x_shape, jax.ShapeDtypeStruct((m, n), BF16), jax.ShapeDtypeStruct((m, 1), F32)],
        casts)


def _attn_block(sink_ref, bias_ref, q_ref, kv_parts, z_ref, o_ref, rows, outside):
    qscale = ATTN_SCALE * LOG2E
    for kv in range(N_KV):
        ksl = slice(kv * HEAD_DIM, (kv + 1) * HEAD_DIM)
        vsl = slice(KV_W + kv * HEAD_DIM, KV_W + (kv + 1) * HEAD_DIM)
        q4 = jnp.concatenate(
            [q_ref[rows, (kv * GROUP + g) * HEAD_DIM:(kv * GROUP + g + 1) * HEAD_DIM]
             for g in range(GROUP)], axis=0)
        kw = jnp.concatenate([r[rs, ksl] for r, rs in kv_parts], axis=0)
        vw = jnp.concatenate([r[rs, vsl] for r, rs in kv_parts], axis=0)
        s = lax.dot_general(q4, kw, (((1,), (1,)), ((), ())), preferred_element_type=F32)
        inv, probs = [], []
        for g in range(GROUP):
            head = kv * GROUP + g
            sg = s[g * BLOCK:(g + 1) * BLOCK] * qscale + bias_ref[head]
            if outside is not None:
                sg = jnp.where(outside, MASKED_SCORE, sg)
            sink2 = sink_ref[head] * LOG2E
            mx = jnp.maximum(jnp.max(sg, axis=-1, keepdims=True), sink2)
            e = jnp.exp2(sg - mx)
            denom = jnp.sum(e, axis=-1, keepdims=True) + jnp.exp2(sink2 - mx)
            inv.append(1.0 / denom)
            probs.append(e.astype(BF16))
        o4 = _dot(jnp.concatenate(probs, axis=0), vw)
        for g in range(GROUP):
            sl = slice((kv * GROUP + g) * HEAD_DIM, (kv * GROUP + g + 1) * HEAD_DIM)
            o_ref[rows, sl] = (o4[g * BLOCK:(g + 1) * BLOCK]
                               * (inv[g] * _silu(z_ref[rows, sl]))).astype(o_ref.dtype)


def _attn_kernel(sink_ref, bias_ref, q_ref, kvp_ref, kvc_ref, kvn_ref, z_ref, o_ref):
    bi = pl.program_id(0)
    is_first = bi == 0
    is_last = bi == pl.num_programs(0) - 1
    on_edge = is_first | is_last
    lo, hi = slice(0, BLOCK), slice(BLOCK, 2 * BLOCK)
    parts_lo = [(kvp_ref, lo), (kvc_ref, lo), (kvc_ref, hi)]
    parts_hi = [(kvc_ref, lo), (kvc_ref, hi), (kvn_ref, lo)]

    def both(outside_lo, outside_hi):
        _attn_block(sink_ref, bias_ref, q_ref, parts_lo, z_ref, o_ref, lo, outside_lo)
        _attn_block(sink_ref, bias_ref, q_ref, parts_hi, z_ref, o_ref, hi, outside_hi)

    @pl.when(jnp.logical_not(on_edge))
    def _():
        both(None, None)

    @pl.when(on_edge)
    def _():
        col = lax.broadcasted_iota(jnp.int32, (BLOCK, 3 * BLOCK), 1)
        both(is_first & (col < BLOCK), is_last & (col >= 2 * BLOCK))


def _attention_bias():
    slopes = 2.0 ** (-8.0 * jnp.arange(1, N_HEADS + 1, dtype=F32) / N_HEADS)
    t = jnp.arange(BLOCK)[:, None]
    c = jnp.arange(3 * BLOCK)[None, :]
    dist = jnp.abs(t + BLOCK - c)
    bias = -(slopes * LOG2E)[:, None, None] * dist.astype(F32)[None]
    return jnp.where((dist <= WINDOW)[None], bias, MASKED_SCORE)


def _attention(qkv, z, sink, casts):
    s_len = qkv.shape[0]
    nb = s_len // BLOCK
    assert nb % 2 == 0
    kv_blk = Q_W // (2 * KV_W)
    assert kv_blk * 2 * KV_W == Q_W
    halo = (BLOCK, 2 * KV_W)
    q_spec = pl.BlockSpec((2 * BLOCK, Q_W), lambda bi: (bi, 0))
    (out,), cast = _call(
        _attn_kernel, "banded_attention", (nb // 2,), ("arbitrary",),
        [pl.BlockSpec(memory_space=pltpu.SMEM),
         _resident((N_HEADS, BLOCK, 3 * BLOCK), lambda bi: (0, 0, 0)),
         q_spec,
         pl.BlockSpec(halo, lambda bi: (jnp.maximum(2 * bi - 1, 0), kv_blk)),
         pl.BlockSpec((2 * BLOCK, 2 * KV_W), lambda bi: (bi, kv_blk)),
         pl.BlockSpec(halo, lambda bi: (jnp.minimum(2 * bi + 2, nb - 1), kv_blk)),
         q_spec],
        [sink, _attention_bias(), qkv, qkv, qkv, qkv, z],
        [q_spec],
        [jax.ShapeDtypeStruct((s_len, Q_W), BF16)],
        casts)
    return out, cast


def _window_sum(e, win, rows):
    n = e.shape[0]
    h = POOL_HALO

    def ahead(x, k):
        return pltpu.roll(x, n - k, axis=0)

    def behind(x, k):
        return pltpu.roll(x, k, axis=0)

    if win == 2:
        return (e + ahead(e, 1))[h:h + rows]
    q1 = e + behind(e, 1)
    if win == 4:
        return (q1 + ahead(q1, 2))[h:h + rows]
    q2 = q1 + behind(q1, 2)
    if win == 8:
        return (q2 + ahead(q2, 4))[h:h + rows]
    assert win == 16
    q3 = q2 + behind(q2, 4)
    return q3[h:h + rows] + q3[2 * h:2 * h + rows]


def _pool_group_kernel(vp_ref, vc_ref, vn_ref, w_ref, scale_ref, z_ref, o_ref, d_ref,
                       *, seq_len, win):
    i = pl.program_id(0)
    n_i = pl.num_programs(0)
    tm = vc_ref.shape[0]
    left = (win - 1) // 2
    right = win - 1 - left
    h = POOL_HALO
    n_chunks = tm // POOL_CHUNK
    for c in range(n_chunks):
        r0 = c * POOL_CHUNK
        top = jnp.where(i > 0, vp_ref[...], 0.0) if c == 0 else vc_ref[r0 - h:r0, :]
        bot = (jnp.where(i < n_i - 1, vn_ref[...], 0.0) if c == n_chunks - 1
               else vc_ref[r0 + POOL_CHUNK:r0 + POOL_CHUNK + h, :])
        cur = vc_ref[r0:r0 + POOL_CHUNK, :]
        e = jnp.concatenate([top, cur, bot], axis=0)
        t = i * tm + r0 + lax.broadcasted_iota(jnp.int32, (POOL_CHUNK, 1), 0)
        lo = jnp.maximum(t - left, 0)
        hi = jnp.minimum(t + right + 1, seq_len)
        inv_cnt = 1.0 / (hi - lo).astype(F32)
        d_ref[r0:r0 + POOL_CHUNK, :] = (
            _window_sum(e, win, POOL_CHUNK) * inv_cnt - cur).astype(d_ref.dtype)
    y = _dot(d_ref[...], w_ref[...].astype(BF16)) * scale_ref[...]
    o_ref[...] = (y * _silu(z_ref[...])).astype(o_ref.dtype)


def _pool_group(u, w_grp, layer, scale, grp):
    s_len = u.shape[0]
    gw = w_grp.shape[2]
    tm = _row_tile(s_len, POOL_ROW_TILE)
    assert tm % POOL_CHUNK == 0
    halo_per_tile = tm // POOL_HALO
    n_halo = s_len // POOL_HALO
    return pl.pallas_call(
        functools.partial(_pool_group_kernel, seq_len=s_len, win=POOL_WINDOWS[grp]),
        grid=(s_len // tm,),
        in_specs=[
            pl.BlockSpec((POOL_HALO, gw), lambda i: (jnp.maximum(i * halo_per_tile - 1, 0), grp)),
            pl.BlockSpec((tm, gw), lambda i: (i, grp)),
            pl.BlockSpec((POOL_HALO, gw),
                         lambda i: (jnp.minimum((i + 1) * halo_per_tile, n_halo - 1), grp)),
            _resident((None, None, gw, gw), lambda i: (layer, grp, 0, 0)),
            pl.BlockSpec((1, gw), lambda i: (0, grp)),
            pl.BlockSpec((tm, gw), lambda i: (i, N_POOL_GROUPS + grp)),
        ],
        out_specs=pl.BlockSpec((tm, gw), lambda i: (i, 0)),
        out_shape=jax.ShapeDtypeStruct((s_len, gw), BF16),
        scratch_shapes=[pltpu.VMEM((tm, gw), BF16)],
        compiler_params=_params("arbitrary"),
        name=f"pool_group{grp}",
    )(u, u, u, w_grp, scale.reshape(1, -1), u)


def kernel(x, p, norm_g, attn_w_in, attn_q_norm_g, attn_k_norm_g, attn_sink, attn_w_out,
           pool_w_in, pool_w_grp, pool_scale, pool_w_out, ple_norm_g, ple_w_gate, ple_w_proj):
    b, s_len, d = x.shape
    assert b == 1
    depth = p.shape[0]
    assert depth % 2 == 0
    xf = x.reshape(s_len, d)
    (xg, rinv), (w_in,) = _prep(xf, norm_g[0], [(attn_w_in, (0,))])
    for pair in range(depth // 2):
        i = 2 * pair
        g_cols = jnp.concatenate([jnp.tile(attn_q_norm_g[pair].astype(F32), N_HEADS),
                                  jnp.tile(attn_k_norm_g[pair].astype(F32), N_KV),
                                  jnp.ones((KV_W,), F32)]).reshape(1, QKV_W)
        qkv, (w_out, w_pool_in) = _matmul_headnorm(
            xg, rinv, w_in, g_cols, QKV_W, Q_W + KV_W,
            [(attn_w_out, (pair,)), (pool_w_in, (pair,))])
        z, (w_gate, w_proj) = _matmul_normed(
            xg, rinv, w_in, QKV_W, Q_W, F32, "matmul_attn_gate",
            [(ple_w_gate, (i,)), (ple_w_proj, (i,))])
        gated, (w_pool_out,) = _attention(qkv, z, attn_sink[pair].astype(F32),
                                          [(pool_w_out, (pair,))])
        (xf, xg, rinv), _ = _matmul_residual([gated], w_out, xf, ple_norm_g[i], [])
        (xf, xg, rinv), _ = _per_layer_embed(xf, xg, rinv, p, i, w_gate, w_proj,
                                             norm_g[i + 1], [])
        last = i + 2 == depth
        u, (w_gate, w_proj, *w_next) = _matmul_normed(
            xg, rinv, w_pool_in, 0, w_pool_in.shape[1], F32, "matmul_pool_in",
            [(ple_w_gate, (i + 1,)), (ple_w_proj, (i + 1,))]
            + ([] if last else [(attn_w_in, (pair + 1,))]))
        gated = [_pool_group(u, pool_w_grp, pair, pool_scale[pair], grp)
                 for grp in range(N_POOL_GROUPS)]
        (xf, xg, rinv), _ = _matmul_residual(gated, w_pool_out, xf, ple_norm_g[i + 1], [])
        outs, _ = _per_layer_embed(xf, xg, rinv, p, i + 1, w_gate, w_proj,
                                   None if last else norm_g[i + 2], [])
        if last:
            (xf,) = outs
        else:
            xf, xg, rinv = outs
            (w_in,) = w_next
    return xf.reshape(b, s_len, d)
```

```python
import functools
import math

import jax
import jax.numpy as jnp
from jax import lax
from jax.experimental import pallas as pl
from jax.experimental.pallas import tpu as pltpu

F32 = jnp.float32
BF16 = jnp.bfloat16

HEAD_DIM = 128
N_KV = 8
GROUP = 4
N_HEADS = N_KV * GROUP
Q_W = N_HEADS * HEAD_DIM
KV_W = N_KV * HEAD_DIM
QKV_W = Q_W + 2 * KV_W
WINDOW = 128
BLOCK = 128
ATTN_SCALE = HEAD_DIM ** -0.5
POOL_WINDOWS = (2, 4, 8, 16)
N_POOL_GROUPS = len(POOL_WINDOWS)
EPS = 1e-6
LOG2E = math.log2(math.e)
MASKED_SCORE = -1e30

SUBLANES = 8
LANES = 128
BF16_SUBLANE_PACK = 16
VMEM_LIMIT_BYTES = 58 * 1024 * 1024
POOL_HALO = SUBLANES

ROW_TILE = 1024
COL_TILE = 512
COL_TILE_WIDE_K = 256
COL_TILE_PLAIN = 1024
EPILOGUE_CHUNK = 256
POOL_ROW_TILE = 512
POOL_CHUNK = 128
PREP_ROW_TILE = 512


def _params(*semantics):
    return pltpu.CompilerParams(dimension_semantics=semantics,
                                vmem_limit_bytes=VMEM_LIMIT_BYTES)


def _row_tile(m, want):
    t = min(m, want)
    assert m % t == 0 and t % SUBLANES == 0
    return t


def _dot(a, b):
    return jnp.dot(a, b, preferred_element_type=F32)


def _silu(x):
    h = 0.5 * x
    return h + h * jnp.tanh(h)


def _resident(block_shape, index_map):
    return pl.BlockSpec(block_shape, index_map, pipeline_mode=pl.Buffered(1))


def _call(body, name, grid, semantics, in_specs, operands, out_specs, out_shapes,
          casts=(), scratch_shapes=()):
    n_steps = math.prod(grid)
    strides = [math.prod(grid[d + 1:]) for d in range(len(grid))]
    cast_in, cast_out, cast_shapes, cast_ops = [], [], [], []
    for w, lead in casts:
        rows, cols = w.shape[len(lead):]
        rb = BF16_SUBLANE_PACK
        while rb * n_steps < rows:
            rb *= 2
        assert rows % rb == 0
        last = rows // rb - 1

        def block_of(*idx, last=last):
            return jnp.minimum(sum(i * s for i, s in zip(idx, strides)), last)

        cast_in.append(pl.BlockSpec((None,) * len(lead) + (rb, cols),
                                    lambda *idx, lead=lead, b=block_of: lead + (b(*idx), 0)))
        cast_out.append(pl.BlockSpec((rb, cols), lambda *idx, b=block_of: (b(*idx), 0)))
        cast_shapes.append(jax.ShapeDtypeStruct((rows, cols), BF16))
        cast_ops.append(w)
    n_in, n_out, n_cast = len(in_specs), len(out_specs), len(cast_ops)

    def kernel(*refs):
        ins = refs[:n_in]
        srcs = refs[n_in:n_in + n_cast]
        outs = refs[n_in + n_cast:n_in + n_cast + n_out]
        dsts = refs[n_in + n_cast + n_out:n_in + 2 * n_cast + n_out]
        scratch = refs[n_in + 2 * n_cast + n_out:]
        body(*ins, *outs, *scratch)
        for src, dst in zip(srcs, dsts):
            dst[...] = src[...].astype(dst.dtype)

    results = pl.pallas_call(
        kernel,
        grid=grid,
        in_specs=list(in_specs) + cast_in,
        out_specs=list(out_specs) + cast_out,
        out_shape=list(out_shapes) + cast_shapes,
        scratch_shapes=list(scratch_shapes),
        compiler_params=_params(*semantics),
        name=name,
    )(*operands, *cast_ops)
    return results[:n_out], results[n_out:]


def _col_chunks(width):
    step = min(width, EPILOGUE_CHUNK)
    return [slice(c, c + step) for c in range(0, width, step)]


def _tile_cols(cols, tn):
    start = pl.multiple_of(pl.program_id(1) * tn + cols.start, LANES)
    return pl.ds(start, cols.stop - cols.start)


def _emit_norm_inputs(x_chunks, gain_ref, xg_ref, rinv_ref, d_model):
    j = pl.program_id(1)
    part = None
    for cols, x_new in x_chunks:
        gain = gain_ref[:, _tile_cols(cols, xg_ref.shape[1])]
        xg_ref[:, cols] = (x_new * gain).astype(xg_ref.dtype)
        ss = jnp.sum(x_new * x_new, axis=-1, keepdims=True)
        part = ss if part is None else part + ss

    @pl.when(j == 0)
    def _():
        rinv_ref[...] = part

    @pl.when(j > 0)
    def _():
        rinv_ref[...] += part

    @pl.when(j == pl.num_programs(1) - 1)
    def _():
        rinv_ref[...] = lax.rsqrt(rinv_ref[...] * (1.0 / d_model) + EPS)


def _prep_kernel(x_ref, g_ref, xg_ref, rinv_ref):
    x = x_ref[...]
    xg_ref[...] = (x * g_ref[...]).astype(xg_ref.dtype)
    rinv_ref[...] = lax.rsqrt(jnp.mean(x * x, axis=-1, keepdims=True) + EPS)


def _prep(x, g, casts):
    m, d = x.shape
    tm = _row_tile(m, PREP_ROW_TILE)
    return _call(
        _prep_kernel, "norm_prep", (m // tm,), ("arbitrary",),
        [pl.BlockSpec((tm, d), lambda i: (i, 0)),
         pl.BlockSpec((1, d), lambda i: (0, 0))],
        [x, g.reshape(1, d)],
        [pl.BlockSpec((tm, d), lambda i: (i, 0)),
         pl.BlockSpec((tm, 1), lambda i: (i, 0))],
        [jax.ShapeDtypeStruct((m, d), BF16), jax.ShapeDtypeStruct((m, 1), F32)],
        casts)


def _normed_dot(a_ref, rinv_ref, w_ref):
    return _dot(a_ref[...], w_ref[...]) * rinv_ref[...]


def _mm_kernel(a_ref, rinv_ref, w_ref, o_ref):
    o_ref[...] = _normed_dot(a_ref, rinv_ref, w_ref).astype(o_ref.dtype)


def _matmul_normed(a, rinv, w, col0, n, out_dtype, name, casts):
    m, k = a.shape
    tm, tn = _row_tile(m, ROW_TILE), COL_TILE_PLAIN
    assert n % tn == 0 and col0 % tn == 0
    j0 = col0 // tn
    (out,), cast = _call(
        _mm_kernel, name, (m // tm, n // tn), ("arbitrary", "arbitrary"),
        [pl.BlockSpec((tm, k), lambda i, j: (i, 0)),
         pl.BlockSpec((tm, 1), lambda i, j: (i, 0)),
         pl.BlockSpec((k, tn), lambda i, j: (0, j0 + j))],
        [a, rinv, w],
        [pl.BlockSpec((tm, tn), lambda i, j: (i, j))],
        [jax.ShapeDtypeStruct((m, n), out_dtype)],
        casts)
    return out, cast


def _mm_headnorm_kernel(a_ref, rinv_ref, w_ref, g_ref, o_ref, *, norm_tiles):
    j = pl.program_id(1)

    @pl.when(j < norm_tiles)
    def _():
        for cols in _col_chunks(o_ref.shape[1]):
            acc = _dot(a_ref[...], w_ref[:, cols]) * rinv_ref[...]
            for h in range(acc.shape[1] // HEAD_DIM):
                a = acc[:, h * HEAD_DIM:(h + 1) * HEAD_DIM]
                sl = slice(cols.start + h * HEAD_DIM, cols.start + (h + 1) * HEAD_DIM)
                ms = jnp.mean(a * a, axis=-1, keepdims=True)
                g = g_ref[:, _tile_cols(sl, o_ref.shape[1])]
                o_ref[:, sl] = (a * lax.rsqrt(ms + EPS) * g).astype(o_ref.dtype)

    @pl.when(j >= norm_tiles)
    def _():
        o_ref[...] = _normed_dot(a_ref, rinv_ref, w_ref).astype(o_ref.dtype)


def _matmul_headnorm(a, rinv, w, g_cols, n, norm_cols, casts):
    m, k = a.shape
    tm, tn = _row_tile(m, ROW_TILE), COL_TILE
    assert n % tn == 0 and norm_cols % tn == 0
    (out,), cast = _call(
        functools.partial(_mm_headnorm_kernel, norm_tiles=norm_cols // tn),
        "matmul_headnorm", (m // tm, n // tn), ("arbitrary", "arbitrary"),
        [pl.BlockSpec((tm, k), lambda i, j: (i, 0)),
         pl.BlockSpec((tm, 1), lambda i, j: (i, 0)),
         pl.BlockSpec((k, tn), lambda i, j: (0, j)),
         _resident((1, n), lambda i, j: (0, 0))],
        [a, rinv, w, g_cols],
        [pl.BlockSpec((tm, tn), lambda i, j: (i, j))],
        [jax.ShapeDtypeStruct((m, n), BF16)],
        casts)
    return out, cast


def _mm_residual_kernel(*refs, n_a):
    a_refs = refs[:n_a]
    w_ref, res_ref, gain_ref, x_ref, xg_ref, rinv_ref = refs[n_a:]
    kc = a_refs[0].shape[1]
    x_chunks = []
    for cols in _col_chunks(x_ref.shape[1]):
        acc = res_ref[:, cols]
        for c, a_ref in enumerate(a_refs):
            acc = acc + _dot(a_ref[...], w_ref[c * kc:(c + 1) * kc, cols])
        x_ref[:, cols] = acc
        x_chunks.append((cols, acc))
    _emit_norm_inputs(x_chunks, gain_ref, xg_ref, rinv_ref,
                      x_ref.shape[1] * pl.num_programs(1))


def _matmul_residual(a_list, w, res, next_gain, casts):
    m, kc = a_list[0].shape
    k, n = w.shape
    n_a = len(a_list)
    assert kc * n_a == k
    tm, tn = _row_tile(m, ROW_TILE), (COL_TILE if k <= 4096 else COL_TILE_WIDE_K)
    a_spec = pl.BlockSpec((tm, kc), lambda i, j: (i, 0))
    tile = pl.BlockSpec((tm, tn), lambda i, j: (i, j))
    return _call(
        functools.partial(_mm_residual_kernel, n_a=n_a),
        "matmul_residual", (m // tm, n // tn), ("arbitrary", "arbitrary"),
        [a_spec] * n_a + [pl.BlockSpec((k, tn), lambda i, j: (0, j)),
                          tile,
                          _resident((1, n), lambda i, j: (0, 0))],
        [*a_list, w, res, next_gain.reshape(1, n)],
        [tile, tile, pl.BlockSpec((tm, 1), lambda i, j: (i, 0))],
        [jax.ShapeDtypeStruct((m, n), F32),
         jax.ShapeDtypeStruct((m, n), BF16),
         jax.ShapeDtypeStruct((m, 1), F32)],
        casts)


def _ple_update(a_ref, rinv_ref, wg_ref, p_ref, wp_ref, res_ref, x_ref):
    pb = p_ref[...].astype(BF16)
    half_rinv = 0.5 * rinv_ref[...]
    x_chunks = []
    for cols in _col_chunks(x_ref.shape[1]):
        t = jnp.tanh(_dot(a_ref[...], wg_ref[:, cols]) * half_rinv)
        half_proj = 0.5 * _dot(pb, wp_ref[:, _tile_cols(cols, x_ref.shape[1])])
        x_new = res_ref[:, cols] + (half_proj + half_proj * t)
        x_ref[:, cols] = x_new
        x_chunks.append((cols, x_new))
    return x_chunks


def _ple_kernel(a_ref, rinv_ref, wg_ref, p_ref, wp_ref, res_ref, gain_ref,
                x_ref, xg_ref, rinv_out_ref):
    x_chunks = _ple_update(a_ref, rinv_ref, wg_ref, p_ref, wp_ref, res_ref, x_ref)
    _emit_norm_inputs(x_chunks, gain_ref, xg_ref, rinv_out_ref,
                      x_ref.shape[1] * pl.num_programs(1))


def _ple_last_kernel(a_ref, rinv_ref, wg_ref, p_ref, wp_ref, res_ref, x_ref):
    _ple_update(a_ref, rinv_ref, wg_ref, p_ref, wp_ref, res_ref, x_ref)


def _per_layer_embed(x, xg, rinv, p, layer, w_gate, w_proj, next_gain, casts):
    m, k = xg.shape
    n = w_gate.shape[1]
    kp = p.shape[3]
    tm, tn = _row_tile(m, ROW_TILE), COL_TILE
    tile = pl.BlockSpec((tm, tn), lambda i, j: (i, j))
    col = pl.BlockSpec((tm, 1), lambda i, j: (i, 0))
    in_specs = [pl.BlockSpec((tm, k), lambda i, j: (i, 0)),
                col,
                pl.BlockSpec((k, tn), lambda i, j: (0, j)),
                _resident((None, None, tm, kp), lambda i, j: (layer, 0, i, 0)),
                _resident((kp, n), lambda i, j: (0, 0)),
                tile]
    operands = [xg, rinv, w_gate, p, w_proj, x]
    grid = (m // tm, n // tn)
    sem = ("arbitrary", "arbitrary")
    x_shape = jax.ShapeDtypeStruct((m, n), F32)
    if next_gain is None:
        return _call(_ple_last_kernel, "per_layer_embed_last", grid, sem, in_specs, operands,
                     [tile], [x_shape], casts)
    return _call(
        _ple_kernel, "per_layer_embed", grid, sem,
        in_specs + [_resident((1, n), lambda i, j: (0, 0))],
        operands + [next_gain.reshape(1, n)],
        [tile, tile, col],
        [x_shape, jax.ShapeDtypeStruct((m, n), BF16), jax.ShapeDtypeStruct((m, 1), F32)],
        casts)


def _attn_block(sink_ref, bias_ref, q_ref, kv_parts, z_ref, o_ref, rows, outside):
    qscale = ATTN_SCALE * LOG2E
    for kv in range(N_KV):
        ksl = slice(kv * HEAD_DIM, (kv + 1) * HEAD_DIM)
        vsl = slice(KV_W + kv * HEAD_DIM, KV_W + (kv + 1) * HEAD_DIM)
        q4 = jnp.concatenate(
            [q_ref[rows, (kv * GROUP + g) * HEAD_DIM:(kv * GROUP + g + 1) * HEAD_DIM]
             for g in range(GROUP)], axis=0)
        kw = jnp.concatenate([r[rs, ksl] for r, rs in kv_parts], axis=0)
        vw = jnp.concatenate([r[rs, vsl] for r, rs in kv_parts], axis=0)
        s = lax.dot_general(q4, kw, (((1,), (1,)), ((), ())), preferred_element_type=F32)
        inv, probs = [], []
        for g in range(GROUP):
            head = kv * GROUP + g
            sg = s[g * BLOCK:(g + 1) * BLOCK] * qscale + bias_ref[head]
            if outside is not None:
                sg = jnp.where(outside, MASKED_SCORE, sg)
            sink2 = sink_ref[head] * LOG2E
            mx = jnp.maximum(jnp.max(sg, axis=-1, keepdims=True), sink2)
            e = jnp.exp2(sg - mx)
            denom = jnp.sum(e, axis=-1, keepdims=True) + jnp.exp2(sink2 - mx)
            inv.append(1.0 / denom)
            probs.append(e.astype(BF16))
        o4 = _dot(jnp.concatenate(probs, axis=0), vw)
        for g in range(GROUP):
            sl = slice((kv * GROUP + g) * HEAD_DIM, (kv * GROUP + g + 1) * HEAD_DIM)
            o_ref[rows, sl] = (o4[g * BLOCK:(g + 1) * BLOCK]
                               * (inv[g] * _silu(z_ref[rows, sl]))).astype(o_ref.dtype)


def _attn_kernel(sink_ref, bias_ref, q_ref, kvp_ref, kvc_ref, kvn_ref, z_ref, o_ref):
    bi = pl.program_id(0)
    is_first = bi == 0
    is_last = bi == pl.num_programs(0) - 1
    on_edge = is_first | is_last
    lo, hi = slice(0, BLOCK), slice(BLOCK, 2 * BLOCK)
    parts_lo = [(kvp_ref, lo), (kvc_ref, lo), (kvc_ref, hi)]
    parts_hi = [(kvc_ref, lo), (kvc_ref, hi), (kvn_ref, lo)]

    def both(outside_lo, outside_hi):
        _attn_block(sink_ref, bias_ref, q_ref, parts_lo, z_ref, o_ref, lo, outside_lo)
        _attn_block(sink_ref, bias_ref, q_ref, parts_hi, z_ref, o_ref, hi, outside_hi)

    @pl.when(jnp.logical_not(on_edge))
    def _():
        both(None, None)

    @pl.when(on_edge)
    def _():
        col = lax.broadcasted_iota(jnp.int32, (BLOCK, 3 * BLOCK), 1)
        both(is_first & (col < BLOCK), is_last & (col >= 2 * BLOCK))


def _attention_bias():
    slopes = 2.0 ** (-8.0 * jnp.arange(1, N_HEADS + 1, dtype=F32) / N_HEADS)
    t = jnp.arange(BLOCK)[:, None]
    c = jnp.arange(3 * BLOCK)[None, :]
    dist = jnp.abs(t + BLOCK - c)
    bias = -(slopes * LOG2E)[:, None, None] * dist.astype(F32)[None]
    return jnp.where((dist <= WINDOW)[None], bias, MASKED_SCORE)


def _attention(qkv, z, sink, casts):
    s_len = qkv.shape[0]
    nb = s_len // BLOCK
    assert nb % 2 == 0
    kv_blk = Q_W // (2 * KV_W)
    assert kv_blk * 2 * KV_W == Q_W
    halo = (BLOCK, 2 * KV_W)
    q_spec = pl.BlockSpec((2 * BLOCK, Q_W), lambda bi: (bi, 0))
    (out,), cast = _call(
        _attn_kernel, "banded_attention", (nb // 2,), ("arbitrary",),
        [pl.BlockSpec(memory_space=pltpu.SMEM),
         _resident((N_HEADS, BLOCK, 3 * BLOCK), lambda bi: (0, 0, 0)),
         q_spec,
         pl.BlockSpec(halo, lambda bi: (jnp.maximum(2 * bi - 1, 0), kv_blk)),
         pl.BlockSpec((2 * BLOCK, 2 * KV_W), lambda bi: (bi, kv_blk)),
         pl.BlockSpec(halo, lambda bi: (jnp.minimum(2 * bi + 2, nb - 1), kv_blk)),
         q_spec],
        [sink, _attention_bias(), qkv, qkv, qkv, qkv, z],
        [q_spec],
        [jax.ShapeDtypeStruct((s_len, Q_W), BF16)],
        casts)
    return out, cast


def _window_sum(e, win, rows):
    n = e.shape[0]
    h = POOL_HALO

    def ahead(x, k):
        return pltpu.roll(x, n - k, axis=0)

    def behind(x, k):
        return pltpu.roll(x, k, axis=0)

    if win == 2:
        return (e + ahead(e, 1))[h:h + rows]
    q1 = e + behind(e, 1)
    if win == 4:
        return (q1 + ahead(q1, 2))[h:h + rows]
    q2 = q1 + behind(q1, 2)
    if win == 8:
        return (q2 + ahead(q2, 4))[h:h + rows]
    assert win == 16
    q3 = q2 + behind(q2, 4)
    return q3[h:h + rows] + q3[2 * h:2 * h + rows]


def _pool_group_kernel(vp_ref, vc_ref, vn_ref, w_ref, scale_ref, z_ref, o_ref, d_ref,
                       *, seq_len, win):
    i = pl.program_id(0)
    n_i = pl.num_programs(0)
    tm = vc_ref.shape[0]
    left = (win - 1) // 2
    right = win - 1 - left
    h = POOL_HALO
    n_chunks = tm // POOL_CHUNK
    for c in range(n_chunks):
        r0 = c * POOL_CHUNK
        top = jnp.where(i > 0, vp_ref[...], 0.0) if c == 0 else vc_ref[r0 - h:r0, :]
        bot = (jnp.where(i < n_i - 1, vn_ref[...], 0.0) if c == n_chunks - 1
               else vc_ref[r0 + POOL_CHUNK:r0 + POOL_CHUNK + h, :])
        cur = vc_ref[r0:r0 + POOL_CHUNK, :]
        e = jnp.concatenate([top, cur, bot], axis=0)
        t = i * tm + r0 + lax.broadcasted_iota(jnp.int32, (POOL_CHUNK, 1), 0)
        lo = jnp.maximum(t - left, 0)
        hi = jnp.minimum(t + right + 1, seq_len)
        inv_cnt = 1.0 / (hi - lo).astype(F32)
        d_ref[r0:r0 + POOL_CHUNK, :] = (
            _window_sum(e, win, POOL_CHUNK) * inv_cnt - cur).astype(d_ref.dtype)
    y = _dot(d_ref[...], w_ref[...].astype(BF16)) * scale_ref[...]
    o_ref[...] = (y * _silu(z_ref[...])).astype(o_ref.dtype)


def _pool_group(u, w_grp, layer, scale, grp):
    s_len = u.shape[0]
    gw = w_grp.shape[2]
    tm = _row_tile(s_len, POOL_ROW_TILE)
    assert tm % POOL_CHUNK == 0
    halo_per_tile = tm // POOL_HALO
    n_halo = s_len // POOL_HALO
    return pl.pallas_call(
        functools.partial(_pool_group_kernel, seq_len=s_len, win=POOL_WINDOWS[grp]),
        grid=(s_len // tm,),
        in_specs=[
            pl.BlockSpec((POOL_HALO, gw), lambda i: (jnp.maximum(i * halo_per_tile - 1, 0), grp)),
            pl.BlockSpec((tm, gw), lambda i: (i, grp)),
            pl.BlockSpec((POOL_HALO, gw),
                         lambda i: (jnp.minimum((i + 1) * halo_per_tile, n_halo - 1), grp)),
            _resident((None, None, gw, gw), lambda i: (layer, grp, 0, 0)),
            pl.BlockSpec((1, gw), lambda i: (0, grp)),
            pl.BlockSpec((tm, gw), lambda i: (i, N_POOL_GROUPS + grp)),
        ],
        out_specs=pl.BlockSpec((tm, gw), lambda i: (i, 0)),
        out_shape=jax.ShapeDtypeStruct((s_len, gw), BF16),
        scratch_shapes=[pltpu.VMEM((tm, gw), BF16)],
        compiler_params=_params("arbitrary"),
        name=f"pool_group{grp}",
    )(u, u, u, w_grp, scale.reshape(1, -1), u)


def kernel(x, p, norm_g, attn_w_in, attn_q_norm_g, attn_k_norm_g, attn_sink, attn_w_out,
           pool_w_in, pool_w_grp, pool_scale, pool_w_out, ple_norm_g, ple_w_gate, ple_w_proj):
    b, s_len, d = x.shape
    assert b == 1
    depth = p.shape[0]
    assert depth % 2 == 0
    xf = x.reshape(s_len, d)
    (xg, rinv), (w_in,) = _prep(xf, norm_g[0], [(attn_w_in, (0,))])
    for pair in range(depth // 2):
        i = 2 * pair
        g_cols = jnp.concatenate([jnp.tile(attn_q_norm_g[pair].astype(F32), N_HEADS),
                                  jnp.tile(attn_k_norm_g[pair].astype(F32), N_KV),
                                  jnp.ones((KV_W,), F32)]).reshape(1, QKV_W)
        qkv, (w_out, w_pool_in) = _matmul_headnorm(
            xg, rinv, w_in, g_cols, QKV_W, Q_W + KV_W,
            [(attn_w_out, (pair,)), (pool_w_in, (pair,))])
        z, (w_gate, w_proj) = _matmul_normed(
            xg, rinv, w_in, QKV_W, Q_W, F32, "matmul_attn_gate",
            [(ple_w_gate, (i,)), (ple_w_proj, (i,))])
        gated, (w_pool_out,) = _attention(qkv, z, attn_sink[pair].astype(F32),
                                          [(pool_w_out, (pair,))])
        (xf, xg, rinv), _ = _matmul_residual([gated], w_out, xf, ple_norm_g[i], [])
        (xf, xg, rinv), _ = _per_layer_embed(xf, xg, rinv, p, i, w_gate, w_proj,
                                             norm_g[i + 1], [])
        last = i + 2 == depth
        u, (w_gate, w_proj, *w_next) = _matmul_normed(
            xg, rinv, w_pool_in, 0, w_pool_in.shape[1], F32, "matmul_pool_in",
            [(ple_w_gate, (i + 1,)), (ple_w_proj, (i + 1,))]
            + ([] if last else [(attn_w_in, (pair + 1,))]))
        gated = [_pool_group(u, pool_w_grp, pair, pool_scale[pair], grp)
                 for grp in range(N_POOL_GROUPS)]
        (xf, xg, rinv), _ = _matmul_residual(gated, w_pool_out, xf, ple_norm_g[i + 1], [])
        outs, _ = _per_layer_embed(xf, xg, rinv, p, i + 1, w_gate, w_proj,
                                   None if last else norm_g[i + 2], [])
        if last:
            (xf,) = outs
        else:
            xf, xg, rinv = outs
            (w_in,) = w_next
    return xf.reshape(b, s_len, d)
```

```python
import functools
import math

import jax
import jax.numpy as jnp
from jax import lax
from jax.experimental import pallas as pl
from jax.experimental.pallas import tpu as pltpu

F32 = jnp.float32
BF16 = jnp.bfloat16

HEAD_DIM = 128
N_KV = 8
GROUP = 4
N_HEADS = N_KV * GROUP
Q_W = N_HEADS * HEAD_DIM
KV_W = N_KV * HEAD_DIM
QKV_W = Q_W + 2 * KV_W
WINDOW = 128
BLOCK = 128
ATTN_SCALE = HEAD_DIM ** -0.5
POOL_WINDOWS = (2, 4, 8, 16)
N_POOL_GROUPS = len(POOL_WINDOWS)
EPS = 1e-6
LOG2E = math.log2(math.e)
MASKED_SCORE = -1e30

SUBLANES = 8
LANES = 128
BF16_SUBLANE_PACK = 16
VMEM_LIMIT_BYTES = 58 * 1024 * 1024
POOL_HALO = SUBLANES

ROW_TILE = 1024
COL_TILE = 512
COL_TILE_WIDE_K = 256
COL_TILE_PLAIN = 1024
EPILOGUE_CHUNK = 256
POOL_ROW_TILE = 512
POOL_CHUNK = 128
PREP_ROW_TILE = 512


def _params(*semantics):
    return pltpu.CompilerParams(dimension_semantics=semantics,
                                vmem_limit_bytes=VMEM_LIMIT_BYTES)


def _row_tile(m, want):
    t = min(m, want)
    assert m % t == 0 and t % SUBLANES == 0
    return t


def _dot(a, b):
    return jnp.dot(a, b, preferred_element_type=F32)


def _silu(x):
    h = 0.5 * x
    return h + h * jnp.tanh(h)


def _resident(block_shape, index_map):
    return pl.BlockSpec(block_shape, index_map, pipeline_mode=pl.Buffered(1))


def _call(body, name, grid, semantics, in_specs, operands, out_specs, out_shapes,
          casts=(), scratch_shapes=()):
    n_steps = math.prod(grid)
    strides = [math.prod(grid[d + 1:]) for d in range(len(grid))]
    cast_in, cast_out, cast_shapes, cast_ops = [], [], [], []
    for w, lead, ct in casts:
        rows, cols = w.shape[len(lead):]
        rb = BF16_SUBLANE_PACK
        while rb * n_steps < rows:
            rb *= 2
        assert rows % rb == 0
        last = rows // rb - 1

        def block_of(*idx, last=last):
            return jnp.minimum(sum(i * s for i, s in zip(idx, strides)), last)

        cast_in.append(pl.BlockSpec((None,) * len(lead) + (rb, cols),
                                    lambda *idx, lead=lead, b=block_of: lead + (b(*idx), 0)))
        if ct is None:
            cast_out.append(pl.BlockSpec((rb, cols), lambda *idx, b=block_of: (b(*idx), 0)))
            cast_shapes.append(jax.ShapeDtypeStruct((rows, cols), BF16))
        else:
            assert cols % ct == 0
            cast_out.append(pl.BlockSpec((cols // ct, rb, ct),
                                         lambda *idx, b=block_of: (0, b(*idx), 0)))
            cast_shapes.append(jax.ShapeDtypeStruct((cols // ct, rows, ct), BF16))
        cast_ops.append(w)
    n_in, n_out, n_cast = len(in_specs), len(out_specs), len(cast_ops)

    def kernel(*refs):
        ins = refs[:n_in]
        srcs = refs[n_in:n_in + n_cast]
        outs = refs[n_in + n_cast:n_in + n_cast + n_out]
        dsts = refs[n_in + n_cast + n_out:n_in + 2 * n_cast + n_out]
        scratch = refs[n_in + 2 * n_cast + n_out:]
        body(*ins, *outs, *scratch)
        for src, dst in zip(srcs, dsts):
            if len(dst.shape) == 2:
                dst[...] = src[...].astype(dst.dtype)
            else:
                ct = dst.shape[2]
                for t in range(dst.shape[0]):
                    dst[t] = src[:, t * ct:(t + 1) * ct].astype(dst.dtype)

    results = pl.pallas_call(
        kernel,
        grid=grid,
        in_specs=list(in_specs) + cast_in,
        out_specs=list(out_specs) + cast_out,
        out_shape=list(out_shapes) + cast_shapes,
        scratch_shapes=list(scratch_shapes),
        compiler_params=_params(*semantics),
        name=name,
    )(*operands, *cast_ops)
    return results[:n_out], results[n_out:]


def _col_chunks(width):
    step = min(width, EPILOGUE_CHUNK)
    return [slice(c, c + step) for c in range(0, width, step)]


def _tile_cols(cols, tn):
    start = pl.multiple_of(pl.program_id(1) * tn + cols.start, LANES)
    return pl.ds(start, cols.stop - cols.start)


def _emit_norm_inputs(x_chunks, gain_ref, xg_ref, rinv_ref, d_model):
    j = pl.program_id(1)
    part = None
    for cols, x_new in x_chunks:
        gain = gain_ref[:, _tile_cols(cols, xg_ref.shape[1])]
        xg_ref[:, cols] = (x_new * gain).astype(xg_ref.dtype)
        ss = jnp.sum(x_new * x_new, axis=-1, keepdims=True)
        part = ss if part is None else part + ss

    @pl.when(j == 0)
    def _():
        rinv_ref[...] = part

    @pl.when(j > 0)
    def _():
        rinv_ref[...] += part

    @pl.when(j == pl.num_programs(1) - 1)
    def _():
        rinv_ref[...] = lax.rsqrt(rinv_ref[...] * (1.0 / d_model) + EPS)


def _prep_kernel(x_ref, g_ref, xg_ref, rinv_ref):
    x = x_ref[...]
    xg_ref[...] = (x * g_ref[...]).astype(xg_ref.dtype)
    rinv_ref[...] = lax.rsqrt(jnp.mean(x * x, axis=-1, keepdims=True) + EPS)


def _prep(x, g, casts):
    m, d = x.shape
    tm = _row_tile(m, PREP_ROW_TILE)
    return _call(
        _prep_kernel, "norm_prep", (m // tm,), ("arbitrary",),
        [pl.BlockSpec((tm, d), lambda i: (i, 0)),
         pl.BlockSpec((1, d), lambda i: (0, 0))],
        [x, g.reshape(1, d)],
        [pl.BlockSpec((tm, d), lambda i: (i, 0)),
         pl.BlockSpec((tm, 1), lambda i: (i, 0))],
        [jax.ShapeDtypeStruct((m, d), BF16), jax.ShapeDtypeStruct((m, 1), F32)],
        casts)


def _normed_dot(a_ref, rinv_ref, w_ref):
    return _dot(a_ref[...], w_ref[...]) * rinv_ref[...]


def _mm_kernel(a_ref, rinv_ref, w_ref, o_ref):
    ct = w_ref.shape[2]
    for t in range(w_ref.shape[0]):
        o_ref[:, t * ct:(t + 1) * ct] = (
            _dot(a_ref[...], w_ref[t]) * rinv_ref[...]).astype(o_ref.dtype)


def _matmul_normed(a, rinv, w, col0, n, out_dtype, name, casts):
    m, k = a.shape
    ct = w.shape[2]
    tm, tn = _row_tile(m, ROW_TILE), COL_TILE_PLAIN
    assert n % tn == 0 and col0 % tn == 0 and tn % ct == 0
    j0 = col0 // tn
    (out,), cast = _call(
        _mm_kernel, name, (m // tm, n // tn), ("arbitrary", "arbitrary"),
        [pl.BlockSpec((tm, k), lambda i, j: (i, 0)),
         pl.BlockSpec((tm, 1), lambda i, j: (i, 0)),
         pl.BlockSpec((tn // ct, k, ct), lambda i, j: (j0 + j, 0, 0))],
        [a, rinv, w],
        [pl.BlockSpec((tm, tn), lambda i, j: (i, j))],
        [jax.ShapeDtypeStruct((m, n), out_dtype)],
        casts)
    return out, cast


def _mm_headnorm_kernel(a_ref, rinv_ref, w_ref, g_ref, o_ref, *, norm_tiles):
    j = pl.program_id(1)

    @pl.when(j < norm_tiles)
    def _():
        for cols in _col_chunks(o_ref.shape[1]):
            acc = _dot(a_ref[...], w_ref[:, cols]) * rinv_ref[...]
            for h in range(acc.shape[1] // HEAD_DIM):
                a = acc[:, h * HEAD_DIM:(h + 1) * HEAD_DIM]
                sl = slice(cols.start + h * HEAD_DIM, cols.start + (h + 1) * HEAD_DIM)
                ms = jnp.mean(a * a, axis=-1, keepdims=True)
                g = g_ref[:, _tile_cols(sl, o_ref.shape[1])]
                o_ref[:, sl] = (a * lax.rsqrt(ms + EPS) * g).astype(o_ref.dtype)

    @pl.when(j >= norm_tiles)
    def _():
        o_ref[...] = _normed_dot(a_ref, rinv_ref, w_ref).astype(o_ref.dtype)


def _matmul_headnorm(a, rinv, w, g_cols, n, norm_cols, casts):
    m, k = a.shape
    tm, tn = _row_tile(m, ROW_TILE), COL_TILE
    assert n % tn == 0 and norm_cols % tn == 0 and w.shape[2] == tn
    (out,), cast = _call(
        functools.partial(_mm_headnorm_kernel, norm_tiles=norm_cols // tn),
        "matmul_headnorm", (m // tm, n // tn), ("arbitrary", "arbitrary"),
        [pl.BlockSpec((tm, k), lambda i, j: (i, 0)),
         pl.BlockSpec((tm, 1), lambda i, j: (i, 0)),
         pl.BlockSpec((None, k, tn), lambda i, j: (j, 0, 0)),
         _resident((1, n), lambda i, j: (0, 0))],
        [a, rinv, w, g_cols],
        [pl.BlockSpec((tm, tn), lambda i, j: (i, j))],
        [jax.ShapeDtypeStruct((m, n), BF16)],
        casts)
    return out, cast


def _mm_residual_kernel(*refs, n_a):
    a_refs = refs[:n_a]
    w_ref, res_ref, gain_ref, x_ref, xg_ref, rinv_ref = refs[n_a:]
    kc = a_refs[0].shape[1]
    x_chunks = []
    for cols in _col_chunks(x_ref.shape[1]):
        acc = res_ref[:, cols]
        for c, a_ref in enumerate(a_refs):
            acc = acc + _dot(a_ref[...], w_ref[c * kc:(c + 1) * kc, cols])
        x_ref[:, cols] = acc
        x_chunks.append((cols, acc))
    _emit_norm_inputs(x_chunks, gain_ref, xg_ref, rinv_ref,
                      x_ref.shape[1] * pl.num_programs(1))


def _matmul_residual(a_list, w, res, next_gain, casts):
    m, kc = a_list[0].shape
    n_tiles, k, tn = w.shape
    n = n_tiles * tn
    n_a = len(a_list)
    assert kc * n_a == k
    tm = _row_tile(m, ROW_TILE)
    a_spec = pl.BlockSpec((tm, kc), lambda i, j: (i, 0))
    tile = pl.BlockSpec((tm, tn), lambda i, j: (i, j))
    return _call(
        functools.partial(_mm_residual_kernel, n_a=n_a),
        "matmul_residual", (m // tm, n // tn), ("arbitrary", "arbitrary"),
        [a_spec] * n_a + [pl.BlockSpec((None, k, tn), lambda i, j: (j, 0, 0)),
                          tile,
                          _resident((1, n), lambda i, j: (0, 0))],
        [*a_list, w, res, next_gain.reshape(1, n)],
        [tile, tile, pl.BlockSpec((tm, 1), lambda i, j: (i, 0))],
        [jax.ShapeDtypeStruct((m, n), F32),
         jax.ShapeDtypeStruct((m, n), BF16),
         jax.ShapeDtypeStruct((m, 1), F32)],
        casts)


def _ple_update(a_ref, rinv_ref, wg_ref, p_ref, wp_ref, res_ref, x_ref):
    pb = p_ref[...].astype(BF16)
    half_rinv = 0.5 * rinv_ref[...]
    x_chunks = []
    for cols in _col_chunks(x_ref.shape[1]):
        t = jnp.tanh(_dot(a_ref[...], wg_ref[:, cols]) * half_rinv)
        half_proj = 0.5 * _dot(pb, wp_ref[:, _tile_cols(cols, x_ref.shape[1])])
        x_new = res_ref[:, cols] + (half_proj + half_proj * t)
        x_ref[:, cols] = x_new
        x_chunks.append((cols, x_new))
    return x_chunks


def _ple_kernel(a_ref, rinv_ref, wg_ref, p_ref, wp_ref, res_ref, gain_ref,
                x_ref, xg_ref, rinv_out_ref):
    x_chunks = _ple_update(a_ref, rinv_ref, wg_ref, p_ref, wp_ref, res_ref, x_ref)
    _emit_norm_inputs(x_chunks, gain_ref, xg_ref, rinv_out_ref,
                      x_ref.shape[1] * pl.num_programs(1))


def _ple_last_kernel(a_ref, rinv_ref, wg_ref, p_ref, wp_ref, res_ref, x_ref):
    _ple_update(a_ref, rinv_ref, wg_ref, p_ref, wp_ref, res_ref, x_ref)


def _per_layer_embed(x, xg, rinv, p, layer, w_gate, w_proj, next_gain, casts):
    m, k = xg.shape
    n_tiles, _, tn = w_gate.shape
    n = n_tiles * tn
    kp = p.shape[3]
    tm = _row_tile(m, ROW_TILE)
    tile = pl.BlockSpec((tm, tn), lambda i, j: (i, j))
    col = pl.BlockSpec((tm, 1), lambda i, j: (i, 0))
    in_specs = [pl.BlockSpec((tm, k), lambda i, j: (i, 0)),
                col,
                pl.BlockSpec((None, k, tn), lambda i, j: (j, 0, 0)),
                pl.BlockSpec((None, None, tm, kp), lambda i, j: (layer, 0, i, 0)),
                _resident((kp, n), lambda i, j: (0, 0)),
                tile]
    operands = [xg, rinv, w_gate, p, w_proj, x]
    grid = (m // tm, n // tn)
    sem = ("arbitrary", "arbitrary")
    x_shape = jax.ShapeDtypeStruct((m, n), F32)
    if next_gain is None:
        return _call(_ple_last_kernel, "per_layer_embed_last", grid, sem, in_specs, operands,
                     [tile], [x_shape], casts)
    return _call(
        _ple_kernel, "per_layer_embed", grid, sem,
        in_specs + [_resident((1, n), lambda i, j: (0, 0))],
        operands + [next_gain.reshape(1, n)],
        [tile, tile, col],
        [x_shape, jax.ShapeDtypeStruct((m, n), BF16), jax.ShapeDtypeStruct((m, 1), F32)],
        casts)


def _attn_block(sink_ref, bias_ref, q_ref, kv_parts, z_ref, o_ref, rows, outside):
    qscale = ATTN_SCALE * LOG2E
    for kv in range(N_KV):
        ksl = slice(kv * HEAD_DIM, (kv + 1) * HEAD_DIM)
        vsl = slice(KV_W + kv * HEAD_DIM, KV_W + (kv + 1) * HEAD_DIM)
        q4 = jnp.concatenate(
            [q_ref[rows, (kv * GROUP + g) * HEAD_DIM:(kv * GROUP + g + 1) * HEAD_DIM]
             for g in range(GROUP)], axis=0)
        kw = jnp.concatenate([r[rs, ksl] for r, rs in kv_parts], axis=0)
        vw = jnp.concatenate([r[rs, vsl] for r, rs in kv_parts], axis=0)
        s = lax.dot_general(q4, kw, (((1,), (1,)), ((), ())), preferred_element_type=F32)
        inv, probs = [], []
        for g in range(GROUP):
            head = kv * GROUP + g
            sg = s[g * BLOCK:(g + 1) * BLOCK] * qscale + bias_ref[head]
            if outside is not None:
                sg = jnp.where(outside, MASKED_SCORE, sg)
            sink2 = sink_ref[head] * LOG2E
            mx = jnp.maximum(jnp.max(sg, axis=-1, keepdims=True), sink2)
            e = jnp.exp2(sg - mx)
            denom = jnp.sum(e, axis=-1, keepdims=True) + jnp.exp2(sink2 - mx)
            inv.append(1.0 / denom)
            probs.append(e.astype(BF16))
        o4 = _dot(jnp.concatenate(probs, axis=0), vw)
        for g in range(GROUP):
            sl = slice((kv * GROUP + g) * HEAD_DIM, (kv * GROUP + g + 1) * HEAD_DIM)
            o_ref[rows, sl] = (o4[g * BLOCK:(g + 1) * BLOCK]
                               * (inv[g] * _silu(z_ref[rows, sl]))).astype(o_ref.dtype)


def _attn_kernel(sink_ref, bias_ref, q_ref, kvp_ref, kvc_ref, kvn_ref, z_ref, o_ref):
    bi = pl.program_id(0)
    is_first = bi == 0
    is_last = bi == pl.num_programs(0) - 1
    on_edge = is_first | is_last
    lo, hi = slice(0, BLOCK), slice(BLOCK, 2 * BLOCK)
    parts_lo = [(kvp_ref, lo), (kvc_ref, lo), (kvc_ref, hi)]
    parts_hi = [(kvc_ref, lo), (kvc_ref, hi), (kvn_ref, lo)]

    def both(outside_lo, outside_hi):
        _attn_block(sink_ref, bias_ref, q_ref, parts_lo, z_ref, o_ref, lo, outside_lo)
        _attn_block(sink_ref, bias_ref, q_ref, parts_hi, z_ref, o_ref, hi, outside_hi)

    @pl.when(jnp.logical_not(on_edge))
    def _():
        both(None, None)

    @pl.when(on_edge)
    def _():
        col = lax.broadcasted_iota(jnp.int32, (BLOCK, 3 * BLOCK), 1)
        both(is_first & (col < BLOCK), is_last & (col >= 2 * BLOCK))


def _attention_bias():
    slopes = 2.0 ** (-8.0 * jnp.arange(1, N_HEADS + 1, dtype=F32) / N_HEADS)
    t = jnp.arange(BLOCK)[:, None]
    c = jnp.arange(3 * BLOCK)[None, :]
    dist = jnp.abs(t + BLOCK - c)
    bias = -(slopes * LOG2E)[:, None, None] * dist.astype(F32)[None]
    return jnp.where((dist <= WINDOW)[None], bias, MASKED_SCORE)


def _attention(qkv, z, sink, casts):
    s_len = qkv.shape[0]
    nb = s_len // BLOCK
    assert nb % 2 == 0
    kv_blk = Q_W // (2 * KV_W)
    assert kv_blk * 2 * KV_W == Q_W
    halo = (BLOCK, 2 * KV_W)
    q_spec = pl.BlockSpec((2 * BLOCK, Q_W), lambda bi: (bi, 0))
    (out,), cast = _call(
        _attn_kernel, "banded_attention", (nb // 2,), ("arbitrary",),
        [pl.BlockSpec(memory_space=pltpu.SMEM),
         _resident((N_HEADS, BLOCK, 3 * BLOCK), lambda bi: (0, 0, 0)),
         q_spec,
         pl.BlockSpec(halo, lambda bi: (jnp.maximum(2 * bi - 1, 0), kv_blk)),
         pl.BlockSpec((2 * BLOCK, 2 * KV_W), lambda bi: (bi, kv_blk)),
         pl.BlockSpec(halo, lambda bi: (jnp.minimum(2 * bi + 2, nb - 1), kv_blk)),
         q_spec],
        [sink, _attention_bias(), qkv, qkv, qkv, qkv, z],
        [q_spec],
        [jax.ShapeDtypeStruct((s_len, Q_W), BF16)],
        casts)
    return out, cast


def _window_sum(e, win, rows):
    n = e.shape[0]
    h = POOL_HALO

    def ahead(x, k):
        return pltpu.roll(x, n - k, axis=0)

    def behind(x, k):
        return pltpu.roll(x, k, axis=0)

    if win == 2:
        return (e + ahead(e, 1))[h:h + rows]
    q1 = e + behind(e, 1)
    if win == 4:
        return (q1 + ahead(q1, 2))[h:h + rows]
    q2 = q1 + behind(q1, 2)
    if win == 8:
        return (q2 + ahead(q2, 4))[h:h + rows]
    assert win == 16
    q3 = q2 + behind(q2, 4)
    return q3[h:h + rows] + q3[2 * h:2 * h + rows]


def _pool_group_kernel(vp_ref, vc_ref, vn_ref, w_ref, scale_ref, z_ref, o_ref, d_ref,
                       *, seq_len, win):
    i = pl.program_id(0)
    n_i = pl.num_programs(0)
    tm = vc_ref.shape[0]
    left = (win - 1) // 2
    right = win - 1 - left
    h = POOL_HALO
    n_chunks = tm // POOL_CHUNK
    for c in range(n_chunks):
        r0 = c * POOL_CHUNK
        top = jnp.where(i > 0, vp_ref[...], 0.0) if c == 0 else vc_ref[r0 - h:r0, :]
        bot = (jnp.where(i < n_i - 1, vn_ref[...], 0.0) if c == n_chunks - 1
               else vc_ref[r0 + POOL_CHUNK:r0 + POOL_CHUNK + h, :])
        cur = vc_ref[r0:r0 + POOL_CHUNK, :]
        e = jnp.concatenate([top, cur, bot], axis=0)
        t = i * tm + r0 + lax.broadcasted_iota(jnp.int32, (POOL_CHUNK, 1), 0)
        lo = jnp.maximum(t - left, 0)
        hi = jnp.minimum(t + right + 1, seq_len)
        inv_cnt = 1.0 / (hi - lo).astype(F32)
        d_ref[r0:r0 + POOL_CHUNK, :] = (
            _window_sum(e, win, POOL_CHUNK) * inv_cnt - cur).astype(d_ref.dtype)
    y = _dot(d_ref[...], w_ref[...]) * scale_ref[...]
    o_ref[...] = (y * _silu(z_ref[...])).astype(o_ref.dtype)


def _pool_group(u, w_grp, scale, grp):
    s_len = u.shape[0]
    gw = w_grp.shape[1]
    tm = _row_tile(s_len, POOL_ROW_TILE)
    assert tm % POOL_CHUNK == 0
    halo_per_tile = tm // POOL_HALO
    n_halo = s_len // POOL_HALO
    return pl.pallas_call(
        functools.partial(_pool_group_kernel, seq_len=s_len, win=POOL_WINDOWS[grp]),
        grid=(s_len // tm,),
        in_specs=[
            pl.BlockSpec((POOL_HALO, gw), lambda i: (jnp.maximum(i * halo_per_tile - 1, 0), grp)),
            pl.BlockSpec((tm, gw), lambda i: (i, grp)),
            pl.BlockSpec((POOL_HALO, gw),
                         lambda i: (jnp.minimum((i + 1) * halo_per_tile, n_halo - 1), grp)),
            _resident((gw, gw), lambda i: (grp, 0)),
            pl.BlockSpec((1, gw), lambda i: (0, grp)),
            pl.BlockSpec((tm, gw), lambda i: (i, N_POOL_GROUPS + grp)),
        ],
        out_specs=pl.BlockSpec((tm, gw), lambda i: (i, 0)),
        out_shape=jax.ShapeDtypeStruct((s_len, gw), BF16),
        scratch_shapes=[pltpu.VMEM((tm, gw), BF16)],
        compiler_params=_params("arbitrary"),
        name=f"pool_group{grp}",
    )(u, u, u, w_grp, scale.reshape(1, -1), u)


def kernel(x, p, norm_g, attn_w_in, attn_q_norm_g, attn_k_norm_g, attn_sink, attn_w_out,
           pool_w_in, pool_w_grp, pool_scale, pool_w_out, ple_norm_g, ple_w_gate, ple_w_proj):
    b, s_len, d = x.shape
    assert b == 1
    depth = p.shape[0]
    assert depth % 2 == 0
    xf = x.reshape(s_len, d)
    (xg, rinv), (w_in,) = _prep(xf, norm_g[0], [(attn_w_in, (0,), COL_TILE)])
    n_pool, n_grp, gw, _ = pool_w_grp.shape
    w_grp_rows = pool_w_grp.reshape(n_pool, n_grp * gw, gw)
    for pair in range(depth // 2):
        i = 2 * pair
        g_cols = jnp.concatenate([jnp.tile(attn_q_norm_g[pair].astype(F32), N_HEADS),
                                  jnp.tile(attn_k_norm_g[pair].astype(F32), N_KV),
                                  jnp.ones((KV_W,), F32)]).reshape(1, QKV_W)
        qkv, (w_out, w_pool_in) = _matmul_headnorm(
            xg, rinv, w_in, g_cols, QKV_W, Q_W + KV_W,
            [(attn_w_out, (pair,), COL_TILE), (pool_w_in, (pair,), COL_TILE_PLAIN)])
        z, (w_gate, w_proj) = _matmul_normed(
            xg, rinv, w_in, QKV_W, Q_W, F32, "matmul_attn_gate",
            [(ple_w_gate, (i,), COL_TILE), (ple_w_proj, (i,), None)])
        gated, (w_pool_out,) = _attention(qkv, z, attn_sink[pair].astype(F32),
                                          [(pool_w_out, (pair,), COL_TILE_WIDE_K)])
        (xf, xg, rinv), _ = _matmul_residual([gated], w_out, xf, ple_norm_g[i], [])
        (xf, xg, rinv), _ = _per_layer_embed(xf, xg, rinv, p, i, w_gate, w_proj,
                                             norm_g[i + 1], [])
        last = i + 2 == depth
        u, (w_gate, w_proj, w_grp, *w_next) = _matmul_normed(
            xg, rinv, w_pool_in, 0, w_pool_in.shape[0] * w_pool_in.shape[2], F32,
            "matmul_pool_in",
            [(ple_w_gate, (i + 1,), COL_TILE), (ple_w_proj, (i + 1,), None),
             (w_grp_rows, (pair,), None)]
            + ([] if last else [(attn_w_in, (pair + 1,), COL_TILE)]))
        gated = [_pool_group(u, w_grp, pool_scale[pair], grp) for grp in range(N_POOL_GROUPS)]
        (xf, xg, rinv), _ = _matmul_residual(gated, w_pool_out, xf, ple_norm_g[i + 1], [])
        outs, _ = _per_layer_embed(xf, xg, rinv, p, i + 1, w_gate, w_proj,
                                   None if last else norm_g[i + 2], [])
        if last:
            (xf,) = outs
        else:
            xf, xg, rinv = outs
            (w_in,) = w_next
    return xf.reshape(b, s_len, d)
```

```python
import functools
import math

import jax
import jax.numpy as jnp
from jax import lax
from jax.experimental import pallas as pl
from jax.experimental.pallas import tpu as pltpu

F32 = jnp.float32
BF16 = jnp.bfloat16

HEAD_DIM = 128
N_KV = 8
GROUP = 4
N_HEADS = N_KV * GROUP
Q_W = N_HEADS * HEAD_DIM
KV_W = N_KV * HEAD_DIM
QKV_W = Q_W + 2 * KV_W
WINDOW = 128
BLOCK = 128
ATTN_SCALE = HEAD_DIM ** -0.5
POOL_WINDOWS = (2, 4, 8, 16)
N_POOL_GROUPS = len(POOL_WINDOWS)
EPS = 1e-6
LOG2E = math.log2(math.e)
MASKED_SCORE = -1e30

SUBLANES = 8
LANES = 128
BF16_SUBLANE_PACK = 16
VMEM_LIMIT_BYTES = 58 * 1024 * 1024
POOL_HALO = SUBLANES

ROW_TILE = 1024
COL_TILE = 512
MAX_FULL_TILE_K = 4096
COL_TILE_WIDE_K = 256
COL_TILE_PLAIN = 1024
EPILOGUE_CHUNK = 256
POOL_ROW_TILE = 512
POOL_CHUNK = 128
PREP_ROW_TILE = 512


def _params(*semantics):
    return pltpu.CompilerParams(dimension_semantics=semantics,
                                vmem_limit_bytes=VMEM_LIMIT_BYTES)


def _row_tile(m, want):
    t = min(m, want)
    assert m % t == 0 and t % SUBLANES == 0
    return t


def _dot(a, b):
    return jnp.dot(a, b, preferred_element_type=F32)


def _silu(x):
    h = 0.5 * x
    return h + h * jnp.tanh(h)


def _resident(block_shape, index_map):
    return pl.BlockSpec(block_shape, index_map, pipeline_mode=pl.Buffered(1))


def _call(body, name, grid, semantics, in_specs, operands, out_specs, out_shapes,
          casts=(), scratch_shapes=()):
    n_steps = math.prod(grid)
    strides = [math.prod(grid[d + 1:]) for d in range(len(grid))]
    cast_in, cast_out, cast_shapes, cast_ops = [], [], [], []
    for w, lead in casts:
        rows, cols = w.shape[len(lead):]
        rb = BF16_SUBLANE_PACK
        while rb * n_steps < rows:
            rb *= 2
        assert rows % rb == 0
        last = rows // rb - 1

        def block_of(*idx, last=last):
            return jnp.minimum(sum(i * s for i, s in zip(idx, strides)), last)

        cast_in.append(pl.BlockSpec((None,) * len(lead) + (rb, cols),
                                    lambda *idx, lead=lead, b=block_of: lead + (b(*idx), 0)))
        cast_out.append(pl.BlockSpec((rb, cols), lambda *idx, b=block_of: (b(*idx), 0)))
        cast_shapes.append(jax.ShapeDtypeStruct((rows, cols), BF16))
        cast_ops.append(w)
    n_in, n_out, n_cast = len(in_specs), len(out_specs), len(cast_ops)

    def kernel(*refs):
        ins = refs[:n_in]
        srcs = refs[n_in:n_in + n_cast]
        outs = refs[n_in + n_cast:n_in + n_cast + n_out]
        dsts = refs[n_in + n_cast + n_out:n_in + 2 * n_cast + n_out]
        scratch = refs[n_in + 2 * n_cast + n_out:]
        body(*ins, *outs, *scratch)
        for src, dst in zip(srcs, dsts):
            dst[...] = src[...].astype(dst.dtype)

    results = pl.pallas_call(
        kernel,
        grid=grid,
        in_specs=list(in_specs) + cast_in,
        out_specs=list(out_specs) + cast_out,
        out_shape=list(out_shapes) + cast_shapes,
        scratch_shapes=list(scratch_shapes),
        compiler_params=_params(*semantics),
        name=name,
    )(*operands, *cast_ops)
    return results[:n_out], results[n_out:]


def _col_chunks(width):
    step = min(width, EPILOGUE_CHUNK)
    return [slice(c, c + step) for c in range(0, width, step)]


def _tile_cols(cols, tn):
    start = pl.multiple_of(pl.program_id(1) * tn + cols.start, LANES)
    return pl.ds(start, cols.stop - cols.start)


def _emit_norm_inputs(x_chunks, gain_ref, xg_ref, rinv_ref, d_model):
    j = pl.program_id(1)
    part = None
    for cols, x_new in x_chunks:
        gain = gain_ref[:, _tile_cols(cols, xg_ref.shape[1])]
        xg_ref[:, cols] = (x_new * gain).astype(xg_ref.dtype)
        ss = jnp.sum(x_new * x_new, axis=-1, keepdims=True)
        part = ss if part is None else part + ss

    @pl.when(j == 0)
    def _():
        rinv_ref[...] = part

    @pl.when(j > 0)
    def _():
        rinv_ref[...] += part

    @pl.when(j == pl.num_programs(1) - 1)
    def _():
        rinv_ref[...] = lax.rsqrt(rinv_ref[...] * (1.0 / d_model) + EPS)


def _prep_kernel(x_ref, g_ref, xg_ref, rinv_ref):
    x = x_ref[...]
    xg_ref[...] = (x * g_ref[...]).astype(xg_ref.dtype)
    rinv_ref[...] = lax.rsqrt(jnp.mean(x * x, axis=-1, keepdims=True) + EPS)


def _prep(x, g, casts):
    m, d = x.shape
    tm = _row_tile(m, PREP_ROW_TILE)
    return _call(
        _prep_kernel, "norm_prep", (m // tm,), ("arbitrary",),
        [pl.BlockSpec((tm, d), lambda i: (i, 0)),
         pl.BlockSpec((1, d), lambda i: (0, 0))],
        [x, g.reshape(1, d)],
        [pl.BlockSpec((tm, d), lambda i: (i, 0)),
         pl.BlockSpec((tm, 1), lambda i: (i, 0))],
        [jax.ShapeDtypeStruct((m, d), BF16), jax.ShapeDtypeStruct((m, 1), F32)],
        casts)


def _normed_dot(a_ref, rinv_ref, w_ref):
    return _dot(a_ref[...], w_ref[...]) * rinv_ref[...]


def _mm_kernel(a_ref, rinv_ref, w_ref, o_ref):
    o_ref[...] = _normed_dot(a_ref, rinv_ref, w_ref).astype(o_ref.dtype)


def _matmul_normed(a, rinv, w, col0, n, out_dtype, name, casts):
    m, k = a.shape
    tm, tn = _row_tile(m, ROW_TILE), COL_TILE_PLAIN
    assert n % tn == 0 and col0 % tn == 0
    j0 = col0 // tn
    (out,), cast = _call(
        _mm_kernel, name, (m // tm, n // tn), ("arbitrary", "arbitrary"),
        [pl.BlockSpec((tm, k), lambda i, j: (i, 0)),
         pl.BlockSpec((tm, 1), lambda i, j: (i, 0)),
         pl.BlockSpec((k, tn), lambda i, j: (0, j0 + j))],
        [a, rinv, w],
        [pl.BlockSpec((tm, tn), lambda i, j: (i, j))],
        [jax.ShapeDtypeStruct((m, n), out_dtype)],
        casts)
    return out, cast


def _mm_headnorm_kernel(a_ref, rinv_ref, w_ref, g_ref, o_ref, *, norm_tiles):
    j = pl.program_id(1)

    @pl.when(j < norm_tiles)
    def _():
        for cols in _col_chunks(o_ref.shape[1]):
            acc = _dot(a_ref[...], w_ref[:, cols]) * rinv_ref[...]
            for h in range(acc.shape[1] // HEAD_DIM):
                a = acc[:, h * HEAD_DIM:(h + 1) * HEAD_DIM]
                sl = slice(cols.start + h * HEAD_DIM, cols.start + (h + 1) * HEAD_DIM)
                ms = jnp.mean(a * a, axis=-1, keepdims=True)
                g = g_ref[:, _tile_cols(sl, o_ref.shape[1])]
                o_ref[:, sl] = (a * lax.rsqrt(ms + EPS) * g).astype(o_ref.dtype)

    @pl.when(j >= norm_tiles)
    def _():
        o_ref[...] = _normed_dot(a_ref, rinv_ref, w_ref).astype(o_ref.dtype)


def _matmul_headnorm(a, rinv, w, g_cols, norm_cols, casts):
    m, k = a.shape
    n = g_cols.shape[1]
    tm, tn = _row_tile(m, ROW_TILE), COL_TILE
    assert n % tn == 0 and norm_cols % tn == 0
    (out,), cast = _call(
        functools.partial(_mm_headnorm_kernel, norm_tiles=norm_cols // tn),
        "matmul_headnorm", (m // tm, n // tn), ("arbitrary", "arbitrary"),
        [pl.BlockSpec((tm, k), lambda i, j: (i, 0)),
         pl.BlockSpec((tm, 1), lambda i, j: (i, 0)),
         pl.BlockSpec((k, tn), lambda i, j: (0, j)),
         _resident((1, n), lambda i, j: (0, 0))],
        [a, rinv, w, g_cols],
        [pl.BlockSpec((tm, tn), lambda i, j: (i, j))],
        [jax.ShapeDtypeStruct((m, n), BF16)],
        casts)
    return out, cast


def _mm_residual_kernel(*refs, n_a):
    a_refs = refs[:n_a]
    w_ref, res_ref, gain_ref, x_ref, xg_ref, rinv_ref = refs[n_a:]
    kc = a_refs[0].shape[1]
    x_chunks = []
    for cols in _col_chunks(x_ref.shape[1]):
        acc = res_ref[:, cols]
        for c, a_ref in enumerate(a_refs):
            acc = acc + _dot(a_ref[...], w_ref[c * kc:(c + 1) * kc, cols])
        x_ref[:, cols] = acc
        x_chunks.append((cols, acc))
    _emit_norm_inputs(x_chunks, gain_ref, xg_ref, rinv_ref,
                      x_ref.shape[1] * pl.num_programs(1))


def _matmul_residual(a_list, w, res, next_gain, casts):
    m, kc = a_list[0].shape
    k, n = w.shape
    n_a = len(a_list)
    assert kc * n_a == k
    tm, tn = _row_tile(m, ROW_TILE), (COL_TILE if k <= MAX_FULL_TILE_K else COL_TILE_WIDE_K)
    a_spec = pl.BlockSpec((tm, kc), lambda i, j: (i, 0))
    tile = pl.BlockSpec((tm, tn), lambda i, j: (i, j))
    return _call(
        functools.partial(_mm_residual_kernel, n_a=n_a),
        "matmul_residual", (m // tm, n // tn), ("arbitrary", "arbitrary"),
        [a_spec] * n_a + [pl.BlockSpec((k, tn), lambda i, j: (0, j)),
                          tile,
                          _resident((1, n), lambda i, j: (0, 0))],
        [*a_list, w, res, next_gain.reshape(1, n)],
        [tile, tile, pl.BlockSpec((tm, 1), lambda i, j: (i, 0))],
        [jax.ShapeDtypeStruct((m, n), F32),
         jax.ShapeDtypeStruct((m, n), BF16),
         jax.ShapeDtypeStruct((m, 1), F32)],
        casts)


def _ple_update(a_ref, rinv_ref, wg_ref, p_ref, wp_ref, res_ref, x_ref):
    pb = p_ref[...].astype(BF16)
    half_rinv = 0.5 * rinv_ref[...]
    x_chunks = []
    for cols in _col_chunks(x_ref.shape[1]):
        t = jnp.tanh(_dot(a_ref[...], wg_ref[:, cols]) * half_rinv)
        half_proj = 0.5 * _dot(pb, wp_ref[:, _tile_cols(cols, x_ref.shape[1])])
        x_new = res_ref[:, cols] + (half_proj + half_proj * t)
        x_ref[:, cols] = x_new
        x_chunks.append((cols, x_new))
    return x_chunks


def _ple_kernel(a_ref, rinv_ref, wg_ref, p_ref, wp_ref, res_ref, gain_ref,
                x_ref, xg_ref, rinv_out_ref):
    x_chunks = _ple_update(a_ref, rinv_ref, wg_ref, p_ref, wp_ref, res_ref, x_ref)
    _emit_norm_inputs(x_chunks, gain_ref, xg_ref, rinv_out_ref,
                      x_ref.shape[1] * pl.num_programs(1))


def _ple_last_kernel(a_ref, rinv_ref, wg_ref, p_ref, wp_ref, res_ref, x_ref):
    _ple_update(a_ref, rinv_ref, wg_ref, p_ref, wp_ref, res_ref, x_ref)


def _per_layer_embed(x, xg, rinv, p, layer, w_gate, w_proj, next_gain, casts):
    m, k = xg.shape
    n = w_gate.shape[1]
    kp = p.shape[3]
    tm, tn = _row_tile(m, ROW_TILE), COL_TILE
    tile = pl.BlockSpec((tm, tn), lambda i, j: (i, j))
    col = pl.BlockSpec((tm, 1), lambda i, j: (i, 0))
    in_specs = [pl.BlockSpec((tm, k), lambda i, j: (i, 0)),
                col,
                pl.BlockSpec((k, tn), lambda i, j: (0, j)),
                _resident((None, None, tm, kp), lambda i, j: (layer, 0, i, 0)),
                _resident((kp, n), lambda i, j: (0, 0)),
                tile]
    operands = [xg, rinv, w_gate, p, w_proj, x]
    grid = (m // tm, n // tn)
    sem = ("arbitrary", "arbitrary")
    x_shape = jax.ShapeDtypeStruct((m, n), F32)
    if next_gain is None:
        return _call(_ple_last_kernel, "per_layer_embed_last", grid, sem, in_specs, operands,
                     [tile], [x_shape], casts)
    return _call(
        _ple_kernel, "per_layer_embed", grid, sem,
        in_specs + [_resident((1, n), lambda i, j: (0, 0))],
        operands + [next_gain.reshape(1, n)],
        [tile, tile, col],
        [x_shape, jax.ShapeDtypeStruct((m, n), BF16), jax.ShapeDtypeStruct((m, 1), F32)],
        casts)


def _attn_block(sink_ref, bias_ref, q_ref, kv_parts, z_ref, o_ref, rows, outside):
    qscale = ATTN_SCALE * LOG2E
    for kv in range(N_KV):
        ksl = slice(kv * HEAD_DIM, (kv + 1) * HEAD_DIM)
        vsl = slice(KV_W + kv * HEAD_DIM, KV_W + (kv + 1) * HEAD_DIM)
        q4 = jnp.concatenate(
            [q_ref[rows, (kv * GROUP + g) * HEAD_DIM:(kv * GROUP + g + 1) * HEAD_DIM]
             for g in range(GROUP)], axis=0)
        kw = jnp.concatenate([r[rs, ksl] for r, rs in kv_parts], axis=0)
        vw = jnp.concatenate([r[rs, vsl] for r, rs in kv_parts], axis=0)
        s = lax.dot_general(q4, kw, (((1,), (1,)), ((), ())), preferred_element_type=F32)
        inv, probs = [], []
        for g in range(GROUP):
            head = kv * GROUP + g
            sg = s[g * BLOCK:(g + 1) * BLOCK] * qscale + bias_ref[head]
            if outside is not None:
                sg = jnp.where(outside, MASKED_SCORE, sg)
            sink2 = sink_ref[head] * LOG2E
            mx = jnp.maximum(jnp.max(sg, axis=-1, keepdims=True), sink2)
            e = jnp.exp2(sg - mx)
            denom = jnp.sum(e, axis=-1, keepdims=True) + jnp.exp2(sink2 - mx)
            inv.append(1.0 / denom)
            probs.append(e.astype(BF16))
        o4 = _dot(jnp.concatenate(probs, axis=0), vw)
        for g in range(GROUP):
            sl = slice((kv * GROUP + g) * HEAD_DIM, (kv * GROUP + g + 1) * HEAD_DIM)
            o_ref[rows, sl] = (o4[g * BLOCK:(g + 1) * BLOCK]
                               * (inv[g] * _silu(z_ref[rows, sl]))).astype(o_ref.dtype)


def _attn_kernel(sink_ref, bias_ref, q_ref, kvp_ref, kvc_ref, kvn_ref, z_ref, o_ref):
    bi = pl.program_id(0)
    is_first = bi == 0
    is_last = bi == pl.num_programs(0) - 1
    on_edge = is_first | is_last
    lo, hi = slice(0, BLOCK), slice(BLOCK, 2 * BLOCK)
    parts_lo = [(kvp_ref, lo), (kvc_ref, lo), (kvc_ref, hi)]
    parts_hi = [(kvc_ref, lo), (kvc_ref, hi), (kvn_ref, lo)]

    def both(outside_lo, outside_hi):
        _attn_block(sink_ref, bias_ref, q_ref, parts_lo, z_ref, o_ref, lo, outside_lo)
        _attn_block(sink_ref, bias_ref, q_ref, parts_hi, z_ref, o_ref, hi, outside_hi)

    @pl.when(jnp.logical_not(on_edge))
    def _():
        both(None, None)

    @pl.when(on_edge)
    def _():
        col = lax.broadcasted_iota(jnp.int32, (BLOCK, 3 * BLOCK), 1)
        both(is_first & (col < BLOCK), is_last & (col >= 2 * BLOCK))


def _attention_bias():
    slopes = 2.0 ** (-8.0 * jnp.arange(1, N_HEADS + 1, dtype=F32) / N_HEADS)
    t = jnp.arange(BLOCK)[:, None]
    c = jnp.arange(3 * BLOCK)[None, :]
    dist = jnp.abs(t + BLOCK - c)
    bias = -(slopes * LOG2E)[:, None, None] * dist.astype(F32)[None]
    return jnp.where((dist <= WINDOW)[None], bias, MASKED_SCORE)


def _attention(qkv, z, sink, casts):
    s_len = qkv.shape[0]
    nb = s_len // BLOCK
    assert nb % 2 == 0
    kv_blk = Q_W // (2 * KV_W)
    assert kv_blk * 2 * KV_W == Q_W
    halo = (BLOCK, 2 * KV_W)
    q_spec = pl.BlockSpec((2 * BLOCK, Q_W), lambda bi: (bi, 0))
    (out,), cast = _call(
        _attn_kernel, "banded_attention", (nb // 2,), ("arbitrary",),
        [pl.BlockSpec(memory_space=pltpu.SMEM),
         _resident((N_HEADS, BLOCK, 3 * BLOCK), lambda bi: (0, 0, 0)),
         q_spec,
         pl.BlockSpec(halo, lambda bi: (jnp.maximum(2 * bi - 1, 0), kv_blk)),
         pl.BlockSpec((2 * BLOCK, 2 * KV_W), lambda bi: (bi, kv_blk)),
         pl.BlockSpec(halo, lambda bi: (jnp.minimum(2 * bi + 2, nb - 1), kv_blk)),
         q_spec],
        [sink, _attention_bias(), qkv, qkv, qkv, qkv, z],
        [q_spec],
        [jax.ShapeDtypeStruct((s_len, Q_W), BF16)],
        casts)
    return out, cast


def _window_sum(e, win, rows):
    n = e.shape[0]
    h = POOL_HALO

    def ahead(x, k):
        return pltpu.roll(x, n - k, axis=0)

    def behind(x, k):
        return pltpu.roll(x, k, axis=0)

    if win == 2:
        return (e + ahead(e, 1))[h:h + rows]
    q1 = e + behind(e, 1)
    if win == 4:
        return (q1 + ahead(q1, 2))[h:h + rows]
    q2 = q1 + behind(q1, 2)
    if win == 8:
        return (q2 + ahead(q2, 4))[h:h + rows]
    assert win == 16
    q3 = q2 + behind(q2, 4)
    return q3[h:h + rows] + q3[2 * h:2 * h + rows]


def _pool_group_kernel(vp_ref, vc_ref, vn_ref, w_ref, scale_ref, z_ref, o_ref, d_ref,
                       *, seq_len, win):
    i = pl.program_id(0)
    n_i = pl.num_programs(0)
    tm = vc_ref.shape[0]
    left = (win - 1) // 2
    right = win - 1 - left
    h = POOL_HALO
    n_chunks = tm // POOL_CHUNK
    for c in range(n_chunks):
        r0 = c * POOL_CHUNK
        top = jnp.where(i > 0, vp_ref[...], 0.0) if c == 0 else vc_ref[r0 - h:r0, :]
        bot = (jnp.where(i < n_i - 1, vn_ref[...], 0.0) if c == n_chunks - 1
               else vc_ref[r0 + POOL_CHUNK:r0 + POOL_CHUNK + h, :])
        cur = vc_ref[r0:r0 + POOL_CHUNK, :]
        e = jnp.concatenate([top, cur, bot], axis=0)
        t = i * tm + r0 + lax.broadcasted_iota(jnp.int32, (POOL_CHUNK, 1), 0)
        lo = jnp.maximum(t - left, 0)
        hi = jnp.minimum(t + right + 1, seq_len)
        inv_cnt = 1.0 / (hi - lo).astype(F32)
        d_ref[r0:r0 + POOL_CHUNK, :] = (
            _window_sum(e, win, POOL_CHUNK) * inv_cnt - cur).astype(d_ref.dtype)
    y = _dot(d_ref[...], w_ref[...].astype(BF16)) * scale_ref[...]
    o_ref[...] = (y * _silu(z_ref[...])).astype(o_ref.dtype)


def _pool_group(u, w_grp, layer, scale, grp):
    s_len = u.shape[0]
    gw = w_grp.shape[2]
    tm = _row_tile(s_len, POOL_ROW_TILE)
    assert tm % POOL_CHUNK == 0
    halo_per_tile = tm // POOL_HALO
    n_halo = s_len // POOL_HALO
    return pl.pallas_call(
        functools.partial(_pool_group_kernel, seq_len=s_len, win=POOL_WINDOWS[grp]),
        grid=(s_len // tm,),
        in_specs=[
            pl.BlockSpec((POOL_HALO, gw), lambda i: (jnp.maximum(i * halo_per_tile - 1, 0), grp)),
            pl.BlockSpec((tm, gw), lambda i: (i, grp)),
            pl.BlockSpec((POOL_HALO, gw),
                         lambda i: (jnp.minimum((i + 1) * halo_per_tile, n_halo - 1), grp)),
            _resident((None, None, gw, gw), lambda i: (layer, grp, 0, 0)),
            pl.BlockSpec((1, gw), lambda i: (0, grp)),
            pl.BlockSpec((tm, gw), lambda i: (i, N_POOL_GROUPS + grp)),
        ],
        out_specs=pl.BlockSpec((tm, gw), lambda i: (i, 0)),
        out_shape=jax.ShapeDtypeStruct((s_len, gw), BF16),
        scratch_shapes=[pltpu.VMEM((tm, gw), BF16)],
        compiler_params=_params("arbitrary"),
        name=f"pool_group{grp}",
    )(u, u, u, w_grp, scale.reshape(1, -1), u)


def kernel(x, p, norm_g, attn_w_in, attn_q_norm_g, attn_k_norm_g, attn_sink, attn_w_out,
           pool_w_in, pool_w_grp, pool_scale, pool_w_out, ple_norm_g, ple_w_gate, ple_w_proj):
    b, s_len, d = x.shape
    assert b == 1
    depth = p.shape[0]
    assert depth % 2 == 0
    xf = x.reshape(s_len, d)
    (xg, rinv), (w_in,) = _prep(xf, norm_g[0], [(attn_w_in, (0,))])
    for pair in range(depth // 2):
        i = 2 * pair
        g_cols = jnp.concatenate([jnp.tile(attn_q_norm_g[pair].astype(F32), N_HEADS),
                                  jnp.tile(attn_k_norm_g[pair].astype(F32), N_KV),
                                  jnp.ones((KV_W,), F32)]).reshape(1, QKV_W)
        qkv, (w_out, w_pool_in) = _matmul_headnorm(
            xg, rinv, w_in, g_cols, Q_W + KV_W,
            [(attn_w_out, (pair,)), (pool_w_in, (pair,))])
        z, (w_gate, w_proj) = _matmul_normed(
            xg, rinv, w_in, QKV_W, Q_W, F32, "matmul_attn_gate",
            [(ple_w_gate, (i,)), (ple_w_proj, (i,))])
        gated, (w_pool_out,) = _attention(qkv, z, attn_sink[pair].astype(F32),
                                          [(pool_w_out, (pair,))])
        (xf, xg, rinv), _ = _matmul_residual([gated], w_out, xf, ple_norm_g[i], [])
        (xf, xg, rinv), _ = _per_layer_embed(xf, xg, rinv, p, i, w_gate, w_proj,
                                             norm_g[i + 1], [])
        last = i + 2 == depth
        u, (w_gate, w_proj, *w_next) = _matmul_normed(
            xg, rinv, w_pool_in, 0, w_pool_in.shape[1], F32, "matmul_pool_in",
            [(ple_w_gate, (i + 1,)), (ple_w_proj, (i + 1,))]
            + ([] if last else [(attn_w_in, (pair + 1,))]))
        gated = [_pool_group(u, pool_w_grp, pair, pool_scale[pair], grp)
                 for grp in range(N_POOL_GROUPS)]
        (xf, xg, rinv), _ = _matmul_residual(gated, w_pool_out, xf, ple_norm_g[i + 1], [])
        outs, _ = _per_layer_embed(xf, xg, rinv, p, i + 1, w_gate, w_proj,
                                   None if last else norm_g[i + 2], [])
        if last:
            (xf,) = outs
        else:
            xf, xg, rinv = outs
            (w_in,) = w_next
    return xf.reshape(b, s_len, d)
```

```python
import functools
import math

import jax
import jax.numpy as jnp
from jax import lax
from jax.experimental import pallas as pl
from jax.experimental.pallas import tpu as pltpu

F32 = jnp.float32
BF16 = jnp.bfloat16

HEAD_DIM = 128
N_KV = 8
GROUP = 4
N_HEADS = N_KV * GROUP
Q_W = N_HEADS * HEAD_DIM
KV_W = N_KV * HEAD_DIM
QKV_W = Q_W + 2 * KV_W
WINDOW = 128
BLOCK = 128
ATTN_SCALE = HEAD_DIM ** -0.5
POOL_WINDOWS = (2, 4, 8, 16)
N_POOL_GROUPS = len(POOL_WINDOWS)
EPS = 1e-6
LOG2E = math.log2(math.e)
SCORE_SCALE = ATTN_SCALE * LOG2E
MASKED_SCORE = -1e30

SUBLANES = 8
LANES = 128
BF16_SUBLANE_PACK = 16
VMEM_LIMIT_BYTES = 58 * 1024 * 1024
POOL_HALO = SUBLANES

ROW_TILE = 1024
COL_TILE = 512
MAX_FULL_TILE_K = 4096
COL_TILE_WIDE_K = 256
COL_TILE_PLAIN = 1024
EPILOGUE_CHUNK = 256
POOL_ROW_TILE = 512
POOL_CHUNK = 128
PREP_ROW_TILE = 512


def _params(*semantics):
    return pltpu.CompilerParams(dimension_semantics=semantics,
                                vmem_limit_bytes=VMEM_LIMIT_BYTES)


def _row_tile(m, want):
    t = min(m, want)
    assert m % t == 0 and t % SUBLANES == 0
    return t


def _dot(a, b):
    return jnp.dot(a, b, preferred_element_type=F32)


def _silu(x):
    h = 0.5 * x
    return h + h * jnp.tanh(h)


def _resident(block_shape, index_map):
    return pl.BlockSpec(block_shape, index_map, pipeline_mode=pl.Buffered(1))


def _call(body, name, grid, semantics, in_specs, operands, out_specs, out_shapes,
          casts=(), scratch_shapes=()):
    n_steps = math.prod(grid)
    strides = [math.prod(grid[d + 1:]) for d in range(len(grid))]
    cast_in, cast_out, cast_shapes, cast_ops = [], [], [], []
    scales = [scale for _, _, scale in casts]
    assert all(math.frexp(scale)[0] == 0.5 for scale in scales)
    for w, lead, _ in casts:
        rows, cols = w.shape[len(lead):]
        rb = BF16_SUBLANE_PACK
        while rb * n_steps < rows:
            rb *= 2
        assert rows % rb == 0
        last = rows // rb - 1

        def block_of(*idx, last=last):
            return jnp.minimum(sum(i * s for i, s in zip(idx, strides)), last)

        cast_in.append(pl.BlockSpec((None,) * len(lead) + (rb, cols),
                                    lambda *idx, lead=lead, b=block_of: lead + (b(*idx), 0)))
        cast_out.append(pl.BlockSpec((rb, cols), lambda *idx, b=block_of: (b(*idx), 0)))
        cast_shapes.append(jax.ShapeDtypeStruct((rows, cols), BF16))
        cast_ops.append(w)
    n_in, n_out, n_cast = len(in_specs), len(out_specs), len(cast_ops)

    def kernel(*refs):
        ins = refs[:n_in]
        srcs = refs[n_in:n_in + n_cast]
        outs = refs[n_in + n_cast:n_in + n_cast + n_out]
        dsts = refs[n_in + n_cast + n_out:n_in + 2 * n_cast + n_out]
        scratch = refs[n_in + 2 * n_cast + n_out:]
        body(*ins, *outs, *scratch)
        for src, dst, scale in zip(srcs, dsts, scales):
            w = src[...] if scale == 1.0 else src[...] * scale
            dst[...] = w.astype(dst.dtype)

    results = pl.pallas_call(
        kernel,
        grid=grid,
        in_specs=list(in_specs) + cast_in,
        out_specs=list(out_specs) + cast_out,
        out_shape=list(out_shapes) + cast_shapes,
        scratch_shapes=list(scratch_shapes),
        compiler_params=_params(*semantics),
        name=name,
    )(*operands, *cast_ops)
    return results[:n_out], results[n_out:]


def _col_chunks(width):
    step = min(width, EPILOGUE_CHUNK)
    return [slice(c, c + step) for c in range(0, width, step)]


def _tile_cols(cols, tn):
    start = pl.multiple_of(pl.program_id(1) * tn + cols.start, LANES)
    return pl.ds(start, cols.stop - cols.start)


def _emit_norm_inputs(x_chunks, gain_ref, xg_ref, rinv_ref, d_model):
    j = pl.program_id(1)
    part = None
    for cols, x_new in x_chunks:
        gain = gain_ref[:, _tile_cols(cols, xg_ref.shape[1])]
        xg_ref[:, cols] = (x_new * gain).astype(xg_ref.dtype)
        ss = jnp.sum(x_new * x_new, axis=-1, keepdims=True)
        part = ss if part is None else part + ss

    @pl.when(j == 0)
    def _():
        rinv_ref[...] = part

    @pl.when(j > 0)
    def _():
        rinv_ref[...] += part

    @pl.when(j == pl.num_programs(1) - 1)
    def _():
        rinv_ref[...] = lax.rsqrt(rinv_ref[...] * (1.0 / d_model) + EPS)


def _prep_kernel(x_ref, g_ref, xg_ref, rinv_ref):
    x = x_ref[...]
    xg_ref[...] = (x * g_ref[...]).astype(xg_ref.dtype)
    rinv_ref[...] = lax.rsqrt(jnp.mean(x * x, axis=-1, keepdims=True) + EPS)


def _prep(x, g, casts):
    m, d = x.shape
    tm = _row_tile(m, PREP_ROW_TILE)
    return _call(
        _prep_kernel, "norm_prep", (m // tm,), ("arbitrary",),
        [pl.BlockSpec((tm, d), lambda i: (i, 0)),
         pl.BlockSpec((1, d), lambda i: (0, 0))],
        [x, g.reshape(1, d)],
        [pl.BlockSpec((tm, d), lambda i: (i, 0)),
         pl.BlockSpec((tm, 1), lambda i: (i, 0))],
        [jax.ShapeDtypeStruct((m, d), BF16), jax.ShapeDtypeStruct((m, 1), F32)],
        casts)


def _normed_dot(a_ref, rinv_ref, w_ref):
    return _dot(a_ref[...], w_ref[...]) * rinv_ref[...]


def _mm_kernel(a_ref, rinv_ref, w_ref, o_ref):
    o_ref[...] = _normed_dot(a_ref, rinv_ref, w_ref).astype(o_ref.dtype)


def _matmul_normed(a, rinv, w, col0, n, out_dtype, name, casts):
    m, k = a.shape
    tm, tn = _row_tile(m, ROW_TILE), COL_TILE_PLAIN
    assert n % tn == 0 and col0 % tn == 0
    j0 = col0 // tn
    (out,), cast = _call(
        _mm_kernel, name, (m // tm, n // tn), ("arbitrary", "arbitrary"),
        [pl.BlockSpec((tm, k), lambda i, j: (i, 0)),
         pl.BlockSpec((tm, 1), lambda i, j: (i, 0)),
         pl.BlockSpec((k, tn), lambda i, j: (0, j0 + j))],
        [a, rinv, w],
        [pl.BlockSpec((tm, tn), lambda i, j: (i, j))],
        [jax.ShapeDtypeStruct((m, n), out_dtype)],
        casts)
    return out, cast


def _mm_headnorm_kernel(a_ref, rinv_ref, w_ref, g_ref, o_ref, *, norm_tiles):
    j = pl.program_id(1)

    @pl.when(j < norm_tiles)
    def _():
        for cols in _col_chunks(o_ref.shape[1]):
            acc = _dot(a_ref[...], w_ref[:, cols]) * rinv_ref[...]
            for h in range(acc.shape[1] // HEAD_DIM):
                a = acc[:, h * HEAD_DIM:(h + 1) * HEAD_DIM]
                sl = slice(cols.start + h * HEAD_DIM, cols.start + (h + 1) * HEAD_DIM)
                ms = jnp.mean(a * a, axis=-1, keepdims=True)
                g = g_ref[:, _tile_cols(sl, o_ref.shape[1])]
                o_ref[:, sl] = (a * lax.rsqrt(ms + EPS) * g).astype(o_ref.dtype)

    @pl.when(j >= norm_tiles)
    def _():
        o_ref[...] = _normed_dot(a_ref, rinv_ref, w_ref).astype(o_ref.dtype)


def _matmul_headnorm(a, rinv, w, g_cols, norm_cols, casts):
    m, k = a.shape
    n = g_cols.shape[1]
    tm, tn = _row_tile(m, ROW_TILE), COL_TILE
    assert n % tn == 0 and norm_cols % tn == 0
    (out,), cast = _call(
        functools.partial(_mm_headnorm_kernel, norm_tiles=norm_cols // tn),
        "matmul_headnorm", (m // tm, n // tn), ("arbitrary", "arbitrary"),
        [pl.BlockSpec((tm, k), lambda i, j: (i, 0)),
         pl.BlockSpec((tm, 1), lambda i, j: (i, 0)),
         pl.BlockSpec((k, tn), lambda i, j: (0, j)),
         _resident((1, n), lambda i, j: (0, 0))],
        [a, rinv, w, g_cols],
        [pl.BlockSpec((tm, tn), lambda i, j: (i, j))],
        [jax.ShapeDtypeStruct((m, n), BF16)],
        casts)
    return out, cast


def _mm_residual_kernel(*refs, n_a):
    a_refs = refs[:n_a]
    w_ref, res_ref, gain_ref, x_ref, xg_ref, rinv_ref = refs[n_a:]
    kc = a_refs[0].shape[1]
    x_chunks = []
    for cols in _col_chunks(x_ref.shape[1]):
        acc = res_ref[:, cols]
        for c, a_ref in enumerate(a_refs):
            acc = acc + _dot(a_ref[...], w_ref[c * kc:(c + 1) * kc, cols])
        x_ref[:, cols] = acc
        x_chunks.append((cols, acc))
    _emit_norm_inputs(x_chunks, gain_ref, xg_ref, rinv_ref,
                      x_ref.shape[1] * pl.num_programs(1))


def _matmul_residual(a_list, w, res, next_gain, casts):
    m, kc = a_list[0].shape
    k, n = w.shape
    n_a = len(a_list)
    assert kc * n_a == k
    tm, tn = _row_tile(m, ROW_TILE), (COL_TILE if k <= MAX_FULL_TILE_K else COL_TILE_WIDE_K)
    a_spec = pl.BlockSpec((tm, kc), lambda i, j: (i, 0))
    tile = pl.BlockSpec((tm, tn), lambda i, j: (i, j))
    return _call(
        functools.partial(_mm_residual_kernel, n_a=n_a),
        "matmul_residual", (m // tm, n // tn), ("arbitrary", "arbitrary"),
        [a_spec] * n_a + [pl.BlockSpec((k, tn), lambda i, j: (0, j)),
                          tile,
                          _resident((1, n), lambda i, j: (0, 0))],
        [*a_list, w, res, next_gain.reshape(1, n)],
        [tile, tile, pl.BlockSpec((tm, 1), lambda i, j: (i, 0))],
        [jax.ShapeDtypeStruct((m, n), F32),
         jax.ShapeDtypeStruct((m, n), BF16),
         jax.ShapeDtypeStruct((m, 1), F32)],
        casts)


def _ple_update(a_ref, rinv_ref, wg_ref, p_ref, wp_ref, res_ref, x_ref):
    pb = p_ref[...].astype(BF16)
    half_rinv = 0.5 * rinv_ref[...]
    x_chunks = []
    for cols in _col_chunks(x_ref.shape[1]):
        t = jnp.tanh(_dot(a_ref[...], wg_ref[:, cols]) * half_rinv)
        half_proj = _dot(pb, wp_ref[:, _tile_cols(cols, x_ref.shape[1])])
        x_new = res_ref[:, cols] + (half_proj + half_proj * t)
        x_ref[:, cols] = x_new
        x_chunks.append((cols, x_new))
    return x_chunks


def _ple_kernel(a_ref, rinv_ref, wg_ref, p_ref, wp_ref, res_ref, gain_ref,
                x_ref, xg_ref, rinv_out_ref):
    x_chunks = _ple_update(a_ref, rinv_ref, wg_ref, p_ref, wp_ref, res_ref, x_ref)
    _emit_norm_inputs(x_chunks, gain_ref, xg_ref, rinv_out_ref,
                      x_ref.shape[1] * pl.num_programs(1))


def _ple_last_kernel(a_ref, rinv_ref, wg_ref, p_ref, wp_ref, res_ref, x_ref):
    _ple_update(a_ref, rinv_ref, wg_ref, p_ref, wp_ref, res_ref, x_ref)


def _per_layer_embed(x, xg, rinv, p, layer, w_gate, w_proj, next_gain, casts):
    m, k = xg.shape
    n = w_gate.shape[1]
    kp = p.shape[3]
    tm, tn = _row_tile(m, ROW_TILE), COL_TILE
    tile = pl.BlockSpec((tm, tn), lambda i, j: (i, j))
    col = pl.BlockSpec((tm, 1), lambda i, j: (i, 0))
    in_specs = [pl.BlockSpec((tm, k), lambda i, j: (i, 0)),
                col,
                pl.BlockSpec((k, tn), lambda i, j: (0, j)),
                pl.BlockSpec((None, None, tm, kp), lambda i, j: (layer, 0, i, 0)),
                _resident((kp, n), lambda i, j: (0, 0)),
                tile]
    operands = [xg, rinv, w_gate, p, w_proj, x]
    grid = (m // tm, n // tn)
    sem = ("arbitrary", "arbitrary")
    x_shape = jax.ShapeDtypeStruct((m, n), F32)
    if next_gain is None:
        return _call(_ple_last_kernel, "per_layer_embed_last", grid, sem, in_specs, operands,
                     [tile], [x_shape], casts)
    return _call(
        _ple_kernel, "per_layer_embed", grid, sem,
        in_specs + [_resident((1, n), lambda i, j: (0, 0))],
        operands + [next_gain.reshape(1, n)],
        [tile, tile, col],
        [x_shape, jax.ShapeDtypeStruct((m, n), BF16), jax.ShapeDtypeStruct((m, 1), F32)],
        casts)


def _attn_block(sink_ref, bias_ref, q_ref, kv_parts, z_ref, o_ref, rows, outside):
    for kv in range(N_KV):
        ksl = slice(kv * HEAD_DIM, (kv + 1) * HEAD_DIM)
        vsl = slice(KV_W + kv * HEAD_DIM, KV_W + (kv + 1) * HEAD_DIM)
        q4 = jnp.concatenate(
            [q_ref[rows, (kv * GROUP + g) * HEAD_DIM:(kv * GROUP + g + 1) * HEAD_DIM]
             for g in range(GROUP)], axis=0)
        kw = jnp.concatenate([r[rs, ksl] for r, rs in kv_parts], axis=0)
        vw = jnp.concatenate([r[rs, vsl] for r, rs in kv_parts], axis=0)
        s = lax.dot_general(q4, kw, (((1,), (1,)), ((), ())), preferred_element_type=F32)
        inv, probs = [], []
        for g in range(GROUP):
            head = kv * GROUP + g
            sg = s[g * BLOCK:(g + 1) * BLOCK] + bias_ref[head]
            if outside is not None:
                sg = jnp.where(outside, MASKED_SCORE, sg)
            sink2 = sink_ref[head] * LOG2E
            mx = jnp.maximum(jnp.max(sg, axis=-1, keepdims=True), sink2)
            e = jnp.exp2(sg - mx)
            denom = jnp.sum(e, axis=-1, keepdims=True) + jnp.exp2(sink2 - mx)
            inv.append(1.0 / denom)
            probs.append(e.astype(BF16))
        o4 = _dot(jnp.concatenate(probs, axis=0), vw)
        for g in range(GROUP):
            sl = slice((kv * GROUP + g) * HEAD_DIM, (kv * GROUP + g + 1) * HEAD_DIM)
            o_ref[rows, sl] = (o4[g * BLOCK:(g + 1) * BLOCK]
                               * (inv[g] * _silu(z_ref[rows, sl]))).astype(o_ref.dtype)


def _attn_kernel(sink_ref, bias_ref, q_ref, kvp_ref, kvc_ref, kvn_ref, z_ref, o_ref):
    bi = pl.program_id(0)
    is_first = bi == 0
    is_last = bi == pl.num_programs(0) - 1
    on_edge = is_first | is_last
    lo, hi = slice(0, BLOCK), slice(BLOCK, 2 * BLOCK)
    parts_lo = [(kvp_ref, lo), (kvc_ref, lo), (kvc_ref, hi)]
    parts_hi = [(kvc_ref, lo), (kvc_ref, hi), (kvn_ref, lo)]

    def both(outside_lo, outside_hi):
        _attn_block(sink_ref, bias_ref, q_ref, parts_lo, z_ref, o_ref, lo, outside_lo)
        _attn_block(sink_ref, bias_ref, q_ref, parts_hi, z_ref, o_ref, hi, outside_hi)

    @pl.when(jnp.logical_not(on_edge))
    def _():
        both(None, None)

    @pl.when(on_edge)
    def _():
        col = lax.broadcasted_iota(jnp.int32, (BLOCK, 3 * BLOCK), 1)
        both(is_first & (col < BLOCK), is_last & (col >= 2 * BLOCK))


def _attention_bias():
    slopes = 2.0 ** (-8.0 * jnp.arange(1, N_HEADS + 1, dtype=F32) / N_HEADS)
    t = jnp.arange(BLOCK)[:, None]
    c = jnp.arange(3 * BLOCK)[None, :]
    dist = jnp.abs(t + BLOCK - c)
    bias = -(slopes * LOG2E)[:, None, None] * dist.astype(F32)[None]
    return jnp.where((dist <= WINDOW)[None], bias, MASKED_SCORE)


def _attention(qkv, z, sink, casts):
    s_len = qkv.shape[0]
    nb = s_len // BLOCK
    assert nb % 2 == 0
    kv_blk = Q_W // (2 * KV_W)
    assert kv_blk * 2 * KV_W == Q_W
    halo = (BLOCK, 2 * KV_W)
    q_spec = pl.BlockSpec((2 * BLOCK, Q_W), lambda bi: (bi, 0))
    (out,), cast = _call(
        _attn_kernel, "banded_attention", (nb // 2,), ("arbitrary",),
        [pl.BlockSpec(memory_space=pltpu.SMEM),
         _resident((N_HEADS, BLOCK, 3 * BLOCK), lambda bi: (0, 0, 0)),
         q_spec,
         pl.BlockSpec(halo, lambda bi: (jnp.maximum(2 * bi - 1, 0), kv_blk)),
         pl.BlockSpec((2 * BLOCK, 2 * KV_W), lambda bi: (bi, kv_blk)),
         pl.BlockSpec(halo, lambda bi: (jnp.minimum(2 * bi + 2, nb - 1), kv_blk)),
         q_spec],
        [sink, _attention_bias(), qkv, qkv, qkv, qkv, z],
        [q_spec],
        [jax.ShapeDtypeStruct((s_len, Q_W), BF16)],
        casts)
    return out, cast


def _window_sum(e, win, rows):
    n = e.shape[0]
    h = POOL_HALO

    def ahead(x, k):
        return pltpu.roll(x, n - k, axis=0)

    def behind(x, k):
        return pltpu.roll(x, k, axis=0)

    if win == 2:
        return (e + ahead(e, 1))[h:h + rows]
    q1 = e + behind(e, 1)
    if win == 4:
        return (q1 + ahead(q1, 2))[h:h + rows]
    q2 = q1 + behind(q1, 2)
    if win == 8:
        return (q2 + ahead(q2, 4))[h:h + rows]
    assert win == 16
    q3 = q2 + behind(q2, 4)
    return q3[h:h + rows] + q3[2 * h:2 * h + rows]


def _pool_group_kernel(vp_ref, vc_ref, vn_ref, w_ref, scale_ref, z_ref, o_ref, d_ref,
                       *, seq_len, win):
    i = pl.program_id(0)
    n_i = pl.num_programs(0)
    tm = vc_ref.shape[0]
    left = (win - 1) // 2
    right = win - 1 - left
    h = POOL_HALO
    n_chunks = tm // POOL_CHUNK
    for c in range(n_chunks):
        r0 = c * POOL_CHUNK
        top = jnp.where(i > 0, vp_ref[...], 0.0) if c == 0 else vc_ref[r0 - h:r0, :]
        bot = (jnp.where(i < n_i - 1, vn_ref[...], 0.0) if c == n_chunks - 1
               else vc_ref[r0 + POOL_CHUNK:r0 + POOL_CHUNK + h, :])
        cur = vc_ref[r0:r0 + POOL_CHUNK, :]
        e = jnp.concatenate([top, cur, bot], axis=0)
        t = i * tm + r0 + lax.broadcasted_iota(jnp.int32, (POOL_CHUNK, 1), 0)
        lo = jnp.maximum(t - left, 0)
        hi = jnp.minimum(t + right + 1, seq_len)
        inv_cnt = 1.0 / (hi - lo).astype(F32)
        d_ref[r0:r0 + POOL_CHUNK, :] = (
            _window_sum(e, win, POOL_CHUNK) * inv_cnt - cur).astype(d_ref.dtype)
    y = _dot(d_ref[...], w_ref[...].astype(BF16)) * scale_ref[...]
    o_ref[...] = (y * _silu(z_ref[...])).astype(o_ref.dtype)


def _pool_group(u, w_grp, layer, scale, grp):
    s_len = u.shape[0]
    gw = w_grp.shape[2]
    tm = _row_tile(s_len, POOL_ROW_TILE)
    assert tm % POOL_CHUNK == 0
    halo_per_tile = tm // POOL_HALO
    n_halo = s_len // POOL_HALO
    return pl.pallas_call(
        functools.partial(_pool_group_kernel, seq_len=s_len, win=POOL_WINDOWS[grp]),
        grid=(s_len // tm,),
        in_specs=[
            pl.BlockSpec((POOL_HALO, gw), lambda i: (jnp.maximum(i * halo_per_tile - 1, 0), grp)),
            pl.BlockSpec((tm, gw), lambda i: (i, grp)),
            pl.BlockSpec((POOL_HALO, gw),
                         lambda i: (jnp.minimum((i + 1) * halo_per_tile, n_halo - 1), grp)),
            _resident((None, None, gw, gw), lambda i: (layer, grp, 0, 0)),
            pl.BlockSpec((1, gw), lambda i: (0, grp)),
            pl.BlockSpec((tm, gw), lambda i: (i, N_POOL_GROUPS + grp)),
        ],
        out_specs=pl.BlockSpec((tm, gw), lambda i: (i, 0)),
        out_shape=jax.ShapeDtypeStruct((s_len, gw), BF16),
        scratch_shapes=[pltpu.VMEM((tm, gw), BF16)],
        compiler_params=_params("arbitrary"),
        name=f"pool_group{grp}",
    )(u, u, u, w_grp, scale.reshape(1, -1), u)


def kernel(x, p, norm_g, attn_w_in, attn_q_norm_g, attn_k_norm_g, attn_sink, attn_w_out,
           pool_w_in, pool_w_grp, pool_scale, pool_w_out, ple_norm_g, ple_w_gate, ple_w_proj):
    b, s_len, d = x.shape
    assert b == 1
    depth = p.shape[0]
    assert depth % 2 == 0
    xf = x.reshape(s_len, d)
    (xg, rinv), (w_in,) = _prep(xf, norm_g[0], [(attn_w_in, (0,), 1.0)])
    for pair in range(depth // 2):
        i = 2 * pair
        g_cols = jnp.concatenate(
            [jnp.tile(attn_q_norm_g[pair].astype(F32) * SCORE_SCALE, N_HEADS),
             jnp.tile(attn_k_norm_g[pair].astype(F32), N_KV),
             jnp.ones((KV_W,), F32)]).reshape(1, QKV_W)
        qkv, (w_out, w_pool_in) = _matmul_headnorm(
            xg, rinv, w_in, g_cols, Q_W + KV_W,
            [(attn_w_out, (pair,), 1.0), (pool_w_in, (pair,), 1.0)])
        z, (w_gate, w_proj) = _matmul_normed(
            xg, rinv, w_in, QKV_W, Q_W, F32, "matmul_attn_gate",
            [(ple_w_gate, (i,), 1.0), (ple_w_proj, (i,), 0.5)])
        gated, _ = _attention(qkv, z, attn_sink[pair].astype(F32), [])
        (xf, xg, rinv), _ = _matmul_residual([gated], w_out, xf, ple_norm_g[i], [])
        (xf, xg, rinv), _ = _per_layer_embed(xf, xg, rinv, p, i, w_gate, w_proj,
                                             norm_g[i + 1], [])
        last = i + 2 == depth
        u, (w_gate, w_proj, w_pool_out, *w_next) = _matmul_normed(
            xg, rinv, w_pool_in, 0, w_pool_in.shape[1], F32, "matmul_pool_in",
            [(ple_w_gate, (i + 1,), 1.0), (ple_w_proj, (i + 1,), 0.5),
             (pool_w_out, (pair,), 1.0)]
            + ([] if last else [(attn_w_in, (pair + 1,), 1.0)]))
        gated = [_pool_group(u, pool_w_grp, pair, pool_scale[pair], grp)
                 for grp in range(N_POOL_GROUPS)]
        (xf, xg, rinv), _ = _matmul_residual(gated, w_pool_out, xf, ple_norm_g[i + 1], [])
        outs, _ = _per_layer_embed(xf, xg, rinv, p, i + 1, w_gate, w_proj,
                                   None if last else norm_g[i + 2], [])
        if last:
            (xf,) = outs
        else:
            xf, xg, rinv = outs
            (w_in,) = w_next
    return xf.reshape(b, s_len, d)
```

```python
import functools
import math

import jax
import jax.numpy as jnp
from jax import lax
from jax.experimental import pallas as pl
from jax.experimental.pallas import tpu as pltpu

F32 = jnp.float32
BF16 = jnp.bfloat16

HEAD_DIM = 128
N_KV = 8
GROUP = 4
N_HEADS = N_KV * GROUP
Q_W = N_HEADS * HEAD_DIM
KV_W = N_KV * HEAD_DIM
QKV_W = Q_W + 2 * KV_W
WINDOW = 128
BLOCK = 128
ATTN_SCALE = HEAD_DIM ** -0.5
POOL_WINDOWS = (2, 4, 8, 16)
N_POOL_GROUPS = len(POOL_WINDOWS)
EPS = 1e-6
LOG2E = math.log2(math.e)
SCORE_SCALE = ATTN_SCALE * LOG2E
MASKED_SCORE = -1e30

SUBLANES = 8
LANES = 128
BF16_SUBLANE_PACK = 16
VMEM_LIMIT_BYTES = 58 * 1024 * 1024
POOL_HALO = SUBLANES

ROW_TILE = 1024
COL_TILE = 512
MAX_FULL_TILE_K = 4096
COL_TILE_WIDE_K = 256
COL_TILE_PLAIN = 1024
EPILOGUE_CHUNK = 256
POOL_ROW_TILE = 512
POOL_CHUNK = 128
PREP_ROW_TILE = 512


def _params(*semantics):
    return pltpu.CompilerParams(dimension_semantics=semantics,
                                vmem_limit_bytes=VMEM_LIMIT_BYTES)


def _row_tile(m, want):
    t = min(m, want)
    assert m % t == 0 and t % SUBLANES == 0
    return t


def _dot(a, b):
    return jnp.dot(a, b, preferred_element_type=F32)


def _silu(x):
    h = 0.5 * x
    return h + h * jnp.tanh(h)


def _resident(block_shape, index_map):
    return pl.BlockSpec(block_shape, index_map, pipeline_mode=pl.Buffered(1))


def _call(body, name, grid, semantics, in_specs, operands, out_specs, out_shapes,
          casts=(), scratch_shapes=()):
    n_steps = math.prod(grid)
    strides = [math.prod(grid[d + 1:]) for d in range(len(grid))]
    cast_in, cast_out, cast_shapes, cast_ops = [], [], [], []
    scales = [scale for _, _, scale in casts]
    assert all(math.frexp(scale)[0] == 0.5 for scale in scales)
    for w, lead, _ in casts:
        rows, cols = w.shape[len(lead):]
        rb = BF16_SUBLANE_PACK
        while rb * n_steps < rows:
            rb *= 2
        assert rows % rb == 0
        last = rows // rb - 1

        def block_of(*idx, last=last):
            return jnp.minimum(sum(i * s for i, s in zip(idx, strides)), last)

        cast_in.append(pl.BlockSpec((None,) * len(lead) + (rb, cols),
                                    lambda *idx, lead=lead, b=block_of: lead + (b(*idx), 0)))
        cast_out.append(pl.BlockSpec((rb, cols), lambda *idx, b=block_of: (b(*idx), 0)))
        cast_shapes.append(jax.ShapeDtypeStruct((rows, cols), BF16))
        cast_ops.append(w)
    n_in, n_out, n_cast = len(in_specs), len(out_specs), len(cast_ops)

    def kernel(*refs):
        ins = refs[:n_in]
        srcs = refs[n_in:n_in + n_cast]
        outs = refs[n_in + n_cast:n_in + n_cast + n_out]
        dsts = refs[n_in + n_cast + n_out:n_in + 2 * n_cast + n_out]
        scratch = refs[n_in + 2 * n_cast + n_out:]
        body(*ins, *outs, *scratch)
        for src, dst, scale in zip(srcs, dsts, scales):
            w = src[...] if scale == 1.0 else src[...] * scale
            dst[...] = w.astype(dst.dtype)

    results = pl.pallas_call(
        kernel,
        grid=grid,
        in_specs=list(in_specs) + cast_in,
        out_specs=list(out_specs) + cast_out,
        out_shape=list(out_shapes) + cast_shapes,
        scratch_shapes=list(scratch_shapes),
        compiler_params=_params(*semantics),
        name=name,
    )(*operands, *cast_ops)
    return results[:n_out], results[n_out:]


def _col_chunks(width):
    step = min(width, EPILOGUE_CHUNK)
    return [slice(c, c + step) for c in range(0, width, step)]


def _tile_cols(cols, tn):
    start = pl.multiple_of(pl.program_id(1) * tn + cols.start, LANES)
    return pl.ds(start, cols.stop - cols.start)


def _emit_norm_inputs(x_chunks, gain_ref, xg_ref, rinv_ref, d_model):
    j = pl.program_id(1)
    part = None
    for cols, x_new in x_chunks:
        gain = gain_ref[:, _tile_cols(cols, xg_ref.shape[1])]
        xg_ref[:, cols] = (x_new * gain).astype(xg_ref.dtype)
        ss = jnp.sum(x_new * x_new, axis=-1, keepdims=True)
        part = ss if part is None else part + ss

    @pl.when(j == 0)
    def _():
        rinv_ref[...] = part

    @pl.when(j > 0)
    def _():
        rinv_ref[...] += part

    @pl.when(j == pl.num_programs(1) - 1)
    def _():
        rinv_ref[...] = lax.rsqrt(rinv_ref[...] * (1.0 / d_model) + EPS)


def _prep_kernel(x_ref, g_ref, xg_ref, rinv_ref):
    x = x_ref[...]
    xg_ref[...] = (x * g_ref[...]).astype(xg_ref.dtype)
    rinv_ref[...] = lax.rsqrt(jnp.mean(x * x, axis=-1, keepdims=True) + EPS)


def _prep(x, g, casts):
    m, d = x.shape
    tm = _row_tile(m, PREP_ROW_TILE)
    return _call(
        _prep_kernel, "norm_prep", (m // tm,), ("arbitrary",),
        [pl.BlockSpec((tm, d), lambda i: (i, 0)),
         pl.BlockSpec((1, d), lambda i: (0, 0))],
        [x, g.reshape(1, d)],
        [pl.BlockSpec((tm, d), lambda i: (i, 0)),
         pl.BlockSpec((tm, 1), lambda i: (i, 0))],
        [jax.ShapeDtypeStruct((m, d), BF16), jax.ShapeDtypeStruct((m, 1), F32)],
        casts)


def _normed_dot(a_ref, rinv_ref, w_ref):
    return _dot(a_ref[...], w_ref[...]) * rinv_ref[...]


def _mm_kernel(a_ref, rinv_ref, w_ref, o_ref):
    o_ref[...] = _normed_dot(a_ref, rinv_ref, w_ref).astype(o_ref.dtype)


def _matmul_normed(a, rinv, w, col0, n, out_dtype, name, casts):
    m, k = a.shape
    tm, tn = _row_tile(m, ROW_TILE), COL_TILE_PLAIN
    assert n % tn == 0 and col0 % tn == 0
    j0 = col0 // tn
    (out,), cast = _call(
        _mm_kernel, name, (m // tm, n // tn), ("arbitrary", "arbitrary"),
        [pl.BlockSpec((tm, k), lambda i, j: (i, 0)),
         pl.BlockSpec((tm, 1), lambda i, j: (i, 0)),
         pl.BlockSpec((k, tn), lambda i, j: (0, j0 + j))],
        [a, rinv, w],
        [pl.BlockSpec((tm, tn), lambda i, j: (i, j))],
        [jax.ShapeDtypeStruct((m, n), out_dtype)],
        casts)
    return out, cast


def _mm_headnorm_kernel(a_ref, rinv_ref, w_ref, g_ref, o_ref, *, norm_tiles):
    j = pl.program_id(1)

    @pl.when(j < norm_tiles)
    def _():
        for cols in _col_chunks(o_ref.shape[1]):
            acc = _dot(a_ref[...], w_ref[:, cols]) * rinv_ref[...]
            for h in range(acc.shape[1] // HEAD_DIM):
                a = acc[:, h * HEAD_DIM:(h + 1) * HEAD_DIM]
                sl = slice(cols.start + h * HEAD_DIM, cols.start + (h + 1) * HEAD_DIM)
                ms = jnp.mean(a * a, axis=-1, keepdims=True)
                g = g_ref[:, _tile_cols(sl, o_ref.shape[1])]
                o_ref[:, sl] = (a * lax.rsqrt(ms + EPS) * g).astype(o_ref.dtype)

    @pl.when(j >= norm_tiles)
    def _():
        o_ref[...] = _normed_dot(a_ref, rinv_ref, w_ref).astype(o_ref.dtype)


def _matmul_headnorm(a, rinv, w, g_cols, norm_cols, casts):
    m, k = a.shape
    n = g_cols.shape[1]
    tm, tn = _row_tile(m, ROW_TILE), COL_TILE
    assert n % tn == 0 and norm_cols % tn == 0
    (out,), cast = _call(
        functools.partial(_mm_headnorm_kernel, norm_tiles=norm_cols // tn),
        "matmul_headnorm", (m // tm, n // tn), ("arbitrary", "arbitrary"),
        [pl.BlockSpec((tm, k), lambda i, j: (i, 0)),
         pl.BlockSpec((tm, 1), lambda i, j: (i, 0)),
         pl.BlockSpec((k, tn), lambda i, j: (0, j)),
         _resident((1, n), lambda i, j: (0, 0))],
        [a, rinv, w, g_cols],
        [pl.BlockSpec((tm, tn), lambda i, j: (i, j))],
        [jax.ShapeDtypeStruct((m, n), BF16)],
        casts)
    return out, cast


def _mm_residual_kernel(*refs, n_a):
    a_refs = refs[:n_a]
    w_ref, res_ref, gain_ref, x_ref, xg_ref, rinv_ref = refs[n_a:]
    kc = a_refs[0].shape[1]
    x_chunks = []
    for cols in _col_chunks(x_ref.shape[1]):
        acc = res_ref[:, cols]
        for c, a_ref in enumerate(a_refs):
            acc = acc + _dot(a_ref[...], w_ref[c * kc:(c + 1) * kc, cols])
        x_ref[:, cols] = acc
        x_chunks.append((cols, acc))
    _emit_norm_inputs(x_chunks, gain_ref, xg_ref, rinv_ref,
                      x_ref.shape[1] * pl.num_programs(1))


def _matmul_residual(a_list, w, res, next_gain, casts):
    m, kc = a_list[0].shape
    k, n = w.shape
    n_a = len(a_list)
    assert kc * n_a == k
    tm, tn = _row_tile(m, ROW_TILE), (COL_TILE if k <= MAX_FULL_TILE_K else COL_TILE_WIDE_K)
    a_spec = pl.BlockSpec((tm, kc), lambda i, j: (i, 0))
    tile = pl.BlockSpec((tm, tn), lambda i, j: (i, j))
    return _call(
        functools.partial(_mm_residual_kernel, n_a=n_a),
        "matmul_residual", (m // tm, n // tn), ("arbitrary", "arbitrary"),
        [a_spec] * n_a + [pl.BlockSpec((k, tn), lambda i, j: (0, j)),
                          tile,
                          _resident((1, n), lambda i, j: (0, 0))],
        [*a_list, w, res, next_gain.reshape(1, n)],
        [tile, tile, pl.BlockSpec((tm, 1), lambda i, j: (i, 0))],
        [jax.ShapeDtypeStruct((m, n), F32),
         jax.ShapeDtypeStruct((m, n), BF16),
         jax.ShapeDtypeStruct((m, 1), F32)],
        casts)


def _ple_update(a_ref, rinv_ref, wg_ref, p_ref, wp_ref, res_ref, x_ref):
    pb = p_ref[...].astype(BF16)
    half_rinv = 0.5 * rinv_ref[...]
    x_chunks = []
    for cols in _col_chunks(x_ref.shape[1]):
        t = jnp.tanh(_dot(a_ref[...], wg_ref[:, cols]) * half_rinv)
        half_proj = _dot(pb, wp_ref[:, _tile_cols(cols, x_ref.shape[1])])
        x_new = res_ref[:, cols] + (half_proj + half_proj * t)
        x_ref[:, cols] = x_new
        x_chunks.append((cols, x_new))
    return x_chunks


def _ple_kernel(a_ref, rinv_ref, wg_ref, p_ref, wp_ref, res_ref, gain_ref,
                x_ref, xg_ref, rinv_out_ref):
    x_chunks = _ple_update(a_ref, rinv_ref, wg_ref, p_ref, wp_ref, res_ref, x_ref)
    _emit_norm_inputs(x_chunks, gain_ref, xg_ref, rinv_out_ref,
                      x_ref.shape[1] * pl.num_programs(1))


def _ple_last_kernel(a_ref, rinv_ref, wg_ref, p_ref, wp_ref, res_ref, x_ref):
    _ple_update(a_ref, rinv_ref, wg_ref, p_ref, wp_ref, res_ref, x_ref)


def _per_layer_embed(x, xg, rinv, p, layer, w_gate, w_proj, next_gain, casts):
    m, k = xg.shape
    n = w_gate.shape[1]
    kp = p.shape[3]
    tm, tn = _row_tile(m, ROW_TILE), COL_TILE
    tile = pl.BlockSpec((tm, tn), lambda i, j: (i, j))
    col = pl.BlockSpec((tm, 1), lambda i, j: (i, 0))
    in_specs = [pl.BlockSpec((tm, k), lambda i, j: (i, 0)),
                col,
                pl.BlockSpec((k, tn), lambda i, j: (0, j)),
                pl.BlockSpec((None, None, tm, kp), lambda i, j: (layer, 0, i, 0)),
                _resident((kp, n), lambda i, j: (0, 0)),
                tile]
    operands = [xg, rinv, w_gate, p, w_proj, x]
    grid = (m // tm, n // tn)
    sem = ("arbitrary", "arbitrary")
    x_shape = jax.ShapeDtypeStruct((m, n), F32)
    if next_gain is None:
        return _call(_ple_last_kernel, "per_layer_embed_last", grid, sem, in_specs, operands,
                     [tile], [x_shape], casts)
    return _call(
        _ple_kernel, "per_layer_embed", grid, sem,
        in_specs + [_resident((1, n), lambda i, j: (0, 0))],
        operands + [next_gain.reshape(1, n)],
        [tile, tile, col],
        [x_shape, jax.ShapeDtypeStruct((m, n), BF16), jax.ShapeDtypeStruct((m, 1), F32)],
        casts)


def _attn_block(sink_ref, bias_ref, q_ref, kv_parts, z_ref, o_ref, rows, outside):
    for kv in range(N_KV):
        ksl = slice(kv * HEAD_DIM, (kv + 1) * HEAD_DIM)
        vsl = slice(KV_W + kv * HEAD_DIM, KV_W + (kv + 1) * HEAD_DIM)
        q4 = jnp.concatenate(
            [q_ref[rows, (kv * GROUP + g) * HEAD_DIM:(kv * GROUP + g + 1) * HEAD_DIM]
             for g in range(GROUP)], axis=0)
        kw = jnp.concatenate([r[rs, ksl] for r, rs in kv_parts], axis=0)
        vw = jnp.concatenate([r[rs, vsl] for r, rs in kv_parts], axis=0)
        s = lax.dot_general(q4, kw, (((1,), (1,)), ((), ())), preferred_element_type=F32)
        inv, probs = [], []
        for g in range(GROUP):
            head = kv * GROUP + g
            sg = s[g * BLOCK:(g + 1) * BLOCK] + bias_ref[head]
            if outside is not None:
                sg = jnp.where(outside, MASKED_SCORE, sg)
            sink2 = sink_ref[head] * LOG2E
            mx = jnp.maximum(jnp.max(sg, axis=-1, keepdims=True), sink2)
            e = jnp.exp2(sg - mx)
            denom = jnp.sum(e, axis=-1, keepdims=True) + jnp.exp2(sink2 - mx)
            inv.append(1.0 / denom)
            probs.append(e.astype(BF16))
        o4 = _dot(jnp.concatenate(probs, axis=0), vw)
        for g in range(GROUP):
            sl = slice((kv * GROUP + g) * HEAD_DIM, (kv * GROUP + g + 1) * HEAD_DIM)
            o_ref[rows, sl] = (o4[g * BLOCK:(g + 1) * BLOCK]
                               * (inv[g] * _silu(z_ref[rows, sl]))).astype(o_ref.dtype)


def _attn_kernel(sink_ref, bias_ref, q_ref, kvp_ref, kvc_ref, kvn_ref, z_ref, o_ref):
    bi = pl.program_id(0)
    is_first = bi == 0
    is_last = bi == pl.num_programs(0) - 1
    on_edge = is_first | is_last
    lo, hi = slice(0, BLOCK), slice(BLOCK, 2 * BLOCK)
    parts_lo = [(kvp_ref, lo), (kvc_ref, lo), (kvc_ref, hi)]
    parts_hi = [(kvc_ref, lo), (kvc_ref, hi), (kvn_ref, lo)]

    def both(outside_lo, outside_hi):
        _attn_block(sink_ref, bias_ref, q_ref, parts_lo, z_ref, o_ref, lo, outside_lo)
        _attn_block(sink_ref, bias_ref, q_ref, parts_hi, z_ref, o_ref, hi, outside_hi)

    @pl.when(jnp.logical_not(on_edge))
    def _():
        both(None, None)

    @pl.when(on_edge)
    def _():
        col = lax.broadcasted_iota(jnp.int32, (BLOCK, 3 * BLOCK), 1)
        both(is_first & (col < BLOCK), is_last & (col >= 2 * BLOCK))


def _attention_bias():
    slopes = 2.0 ** (-8.0 * jnp.arange(1, N_HEADS + 1, dtype=F32) / N_HEADS)
    t = jnp.arange(BLOCK)[:, None]
    c = jnp.arange(3 * BLOCK)[None, :]
    dist = jnp.abs(t + BLOCK - c)
    bias = -(slopes * LOG2E)[:, None, None] * dist.astype(F32)[None]
    return jnp.where((dist <= WINDOW)[None], bias, MASKED_SCORE)


def _attention(qkv, z, sink, casts):
    s_len = qkv.shape[0]
    nb = s_len // BLOCK
    assert nb % 2 == 0
    kv_blk = Q_W // (2 * KV_W)
    assert kv_blk * 2 * KV_W == Q_W
    halo = (BLOCK, 2 * KV_W)
    q_spec = pl.BlockSpec((2 * BLOCK, Q_W), lambda bi: (bi, 0))
    (out,), cast = _call(
        _attn_kernel, "banded_attention", (nb // 2,), ("arbitrary",),
        [pl.BlockSpec(memory_space=pltpu.SMEM),
         _resident((N_HEADS, BLOCK, 3 * BLOCK), lambda bi: (0, 0, 0)),
         q_spec,
         pl.BlockSpec(halo, lambda bi: (jnp.maximum(2 * bi - 1, 0), kv_blk)),
         pl.BlockSpec((2 * BLOCK, 2 * KV_W), lambda bi: (bi, kv_blk)),
         pl.BlockSpec(halo, lambda bi: (jnp.minimum(2 * bi + 2, nb - 1), kv_blk)),
         q_spec],
        [sink, _attention_bias(), qkv, qkv, qkv, qkv, z],
        [q_spec],
        [jax.ShapeDtypeStruct((s_len, Q_W), BF16)],
        casts)
    return out, cast


def _window_sum(e, win, rows):
    n = e.shape[0]
    h = POOL_HALO

    def ahead(x, k):
        return pltpu.roll(x, n - k, axis=0)

    def behind(x, k):
        return pltpu.roll(x, k, axis=0)

    if win == 2:
        return (e + ahead(e, 1))[h:h + rows]
    q1 = e + behind(e, 1)
    if win == 4:
        return (q1 + ahead(q1, 2))[h:h + rows]
    q2 = q1 + behind(q1, 2)
    if win == 8:
        return (q2 + ahead(q2, 4))[h:h + rows]
    assert win == 16
    q3 = q2 + behind(q2, 4)
    return q3[h:h + rows] + q3[2 * h:2 * h + rows]


def _pool_group_kernel(vp_ref, vc_ref, vn_ref, w_ref, scale_ref, z_ref, o_ref,
                       *, seq_len, win):
    i = pl.program_id(0)
    n_i = pl.num_programs(0)
    tm = vc_ref.shape[0]
    left = (win - 1) // 2
    right = win - 1 - left
    h = POOL_HALO
    top = jnp.where(i > 0, vp_ref[...], 0.0)
    bot = jnp.where(i < n_i - 1, vn_ref[...], 0.0)
    a = jnp.concatenate([top, vc_ref[...], bot], axis=0).astype(BF16)
    t = i * tm + lax.broadcasted_iota(jnp.int32, (tm, 1), 0)
    lo = jnp.maximum(t - left, 0)
    hi = jnp.minimum(t + right + 1, seq_len)
    inv_cnt = 1.0 / (hi - lo).astype(F32)
    for c0 in range(0, o_ref.shape[1], COL_TILE):
        cols = slice(c0, c0 + COL_TILE)
        vw = _dot(a, w_ref[:, cols].astype(BF16))
        y = (_window_sum(vw, win, tm) * inv_cnt - vw[h:h + tm]) * scale_ref[:, cols]
        o_ref[:, cols] = (y * _silu(z_ref[:, cols])).astype(o_ref.dtype)


def _pool_group(u, w_grp, layer, scale, grp):
    s_len = u.shape[0]
    gw = w_grp.shape[2]
    tm = _row_tile(s_len, POOL_ROW_TILE)
    assert tm % POOL_CHUNK == 0
    halo_per_tile = tm // POOL_HALO
    n_halo = s_len // POOL_HALO
    return pl.pallas_call(
        functools.partial(_pool_group_kernel, seq_len=s_len, win=POOL_WINDOWS[grp]),
        grid=(s_len // tm,),
        in_specs=[
            pl.BlockSpec((POOL_HALO, gw), lambda i: (jnp.maximum(i * halo_per_tile - 1, 0), grp)),
            pl.BlockSpec((tm, gw), lambda i: (i, grp)),
            pl.BlockSpec((POOL_HALO, gw),
                         lambda i: (jnp.minimum((i + 1) * halo_per_tile, n_halo - 1), grp)),
            _resident((None, None, gw, gw), lambda i: (layer, grp, 0, 0)),
            pl.BlockSpec((1, gw), lambda i: (0, grp)),
            pl.BlockSpec((tm, gw), lambda i: (i, N_POOL_GROUPS + grp)),
        ],
        out_specs=pl.BlockSpec((tm, gw), lambda i: (i, 0)),
        out_shape=jax.ShapeDtypeStruct((s_len, gw), BF16),
        compiler_params=_params("arbitrary"),
        name=f"pool_group{grp}",
    )(u, u, u, w_grp, scale.reshape(1, -1), u)


def kernel(x, p, norm_g, attn_w_in, attn_q_norm_g, attn_k_norm_g, attn_sink, attn_w_out,
           pool_w_in, pool_w_grp, pool_scale, pool_w_out, ple_norm_g, ple_w_gate, ple_w_proj):
    b, s_len, d = x.shape
    assert b == 1
    depth = p.shape[0]
    assert depth % 2 == 0
    xf = x.reshape(s_len, d)
    (xg, rinv), (w_in,) = _prep(xf, norm_g[0], [(attn_w_in, (0,), 1.0)])
    for pair in range(depth // 2):
        i = 2 * pair
        g_cols = jnp.concatenate(
            [jnp.tile(attn_q_norm_g[pair].astype(F32) * SCORE_SCALE, N_HEADS),
             jnp.tile(attn_k_norm_g[pair].astype(F32), N_KV),
             jnp.ones((KV_W,), F32)]).reshape(1, QKV_W)
        qkv, (w_out, w_pool_in) = _matmul_headnorm(
            xg, rinv, w_in, g_cols, Q_W + KV_W,
            [(attn_w_out, (pair,), 1.0), (pool_w_in, (pair,), 1.0)])
        z, (w_gate, w_proj) = _matmul_normed(
            xg, rinv, w_in, QKV_W, Q_W, F32, "matmul_attn_gate",
            [(ple_w_gate, (i,), 1.0), (ple_w_proj, (i,), 0.5)])
        gated, _ = _attention(qkv, z, attn_sink[pair].astype(F32), [])
        (xf, xg, rinv), _ = _matmul_residual([gated], w_out, xf, ple_norm_g[i], [])
        (xf, xg, rinv), _ = _per_layer_embed(xf, xg, rinv, p, i, w_gate, w_proj,
                                             norm_g[i + 1], [])
        last = i + 2 == depth
        u, (w_gate, w_proj, w_pool_out, *w_next) = _matmul_normed(
            xg, rinv, w_pool_in, 0, w_pool_in.shape[1], F32, "matmul_pool_in",
            [(ple_w_gate, (i + 1,), 1.0), (ple_w_proj, (i + 1,), 0.5),
             (pool_w_out, (pair,), 1.0)]
            + ([] if last else [(attn_w_in, (pair + 1,), 1.0)]))
        gated = [_pool_group(u, pool_w_grp, pair, pool_scale[pair], grp)
                 for grp in range(N_POOL_GROUPS)]
        (xf, xg, rinv), _ = _matmul_residual(gated, w_pool_out, xf, ple_norm_g[i + 1], [])
        outs, _ = _per_layer_embed(xf, xg, rinv, p, i + 1, w_gate, w_proj,
                                   None if last else norm_g[i + 2], [])
        if last:
            (xf,) = outs
        else:
            xf, xg, rinv = outs
            (w_in,) = w_next
    return xf.reshape(b, s_len, d)
```

```python
import functools
import math

import jax
import jax.numpy as jnp
from jax import lax
from jax.experimental import pallas as pl
from jax.experimental.pallas import tpu as pltpu

F32 = jnp.float32
BF16 = jnp.bfloat16

HEAD_DIM = 128
N_KV = 8
GROUP = 4
N_HEADS = N_KV * GROUP
Q_W = N_HEADS * HEAD_DIM
KV_W = N_KV * HEAD_DIM
QKV_W = Q_W + 2 * KV_W
WINDOW = 128
BLOCK = 128
ATTN_SCALE = HEAD_DIM ** -0.5
POOL_WINDOWS = (2, 4, 8, 16)
N_POOL_GROUPS = len(POOL_WINDOWS)
EPS = 1e-6
LOG2E = math.log2(math.e)
SCORE_SCALE = ATTN_SCALE * LOG2E
MASKED_SCORE = -1e30

SUBLANES = 8
LANES = 128
BF16_SUBLANE_PACK = 16
VMEM_LIMIT_BYTES = 58 * 1024 * 1024
POOL_HALO = SUBLANES

ROW_TILE = 1024
COL_TILE = 512
MAX_FULL_TILE_K = 4096
COL_TILE_WIDE_K = 256
COL_TILE_PLAIN = 1024
EPILOGUE_CHUNK = 256
POOL_ROW_TILE = 512
POOL_CHUNK = 128
PREP_ROW_TILE = 512


def _params(*semantics):
    return pltpu.CompilerParams(dimension_semantics=semantics,
                                vmem_limit_bytes=VMEM_LIMIT_BYTES)


def _row_tile(m, want):
    t = min(m, want)
    assert m % t == 0 and t % SUBLANES == 0
    return t


def _dot(a, b):
    return jnp.dot(a, b, preferred_element_type=F32)


def _silu(x):
    h = 0.5 * x
    return h + h * jnp.tanh(h)


def _resident(block_shape, index_map):
    return pl.BlockSpec(block_shape, index_map, pipeline_mode=pl.Buffered(1))


def _call(body, name, grid, semantics, in_specs, operands, out_specs, out_shapes,
          casts=(), scratch_shapes=()):
    n_steps = math.prod(grid)
    strides = [math.prod(grid[d + 1:]) for d in range(len(grid))]
    cast_in, cast_out, cast_shapes, cast_ops = [], [], [], []
    scales = [scale for _, _, scale in casts]
    assert all(math.frexp(scale)[0] == 0.5 for scale in scales)
    for w, lead, _ in casts:
        rows, cols = w.shape[len(lead):]
        rb = BF16_SUBLANE_PACK
        while rb * n_steps < rows:
            rb *= 2
        assert rows % rb == 0
        last = rows // rb - 1

        def block_of(*idx, last=last):
            return jnp.minimum(sum(i * s for i, s in zip(idx, strides)), last)

        cast_in.append(pl.BlockSpec((None,) * len(lead) + (rb, cols),
                                    lambda *idx, lead=lead, b=block_of: lead + (b(*idx), 0)))
        cast_out.append(pl.BlockSpec((rb, cols), lambda *idx, b=block_of: (b(*idx), 0)))
        cast_shapes.append(jax.ShapeDtypeStruct((rows, cols), BF16))
        cast_ops.append(w)
    n_in, n_out, n_cast = len(in_specs), len(out_specs), len(cast_ops)

    def kernel(*refs):
        ins = refs[:n_in]
        srcs = refs[n_in:n_in + n_cast]
        outs = refs[n_in + n_cast:n_in + n_cast + n_out]
        dsts = refs[n_in + n_cast + n_out:n_in + 2 * n_cast + n_out]
        scratch = refs[n_in + 2 * n_cast + n_out:]
        body(*ins, *outs, *scratch)
        for src, dst, scale in zip(srcs, dsts, scales):
            w = src[...] if scale == 1.0 else src[...] * scale
            dst[...] = w.astype(dst.dtype)

    results = pl.pallas_call(
        kernel,
        grid=grid,
        in_specs=list(in_specs) + cast_in,
        out_specs=list(out_specs) + cast_out,
        out_shape=list(out_shapes) + cast_shapes,
        scratch_shapes=list(scratch_shapes),
        compiler_params=_params(*semantics),
        name=name,
    )(*operands, *cast_ops)
    return results[:n_out], results[n_out:]


def _col_chunks(width):
    step = min(width, EPILOGUE_CHUNK)
    return [slice(c, c + step) for c in range(0, width, step)]


def _tile_cols(cols, tn):
    start = pl.multiple_of(pl.program_id(1) * tn + cols.start, LANES)
    return pl.ds(start, cols.stop - cols.start)


def _emit_norm_inputs(x_chunks, gain_ref, xg_ref, rinv_ref, d_model):
    j = pl.program_id(1)
    part = None
    for cols, x_new in x_chunks:
        gain = gain_ref[:, _tile_cols(cols, xg_ref.shape[1])]
        xg_ref[:, cols] = (x_new * gain).astype(xg_ref.dtype)
        ss = jnp.sum(x_new * x_new, axis=-1, keepdims=True)
        part = ss if part is None else part + ss

    @pl.when(j == 0)
    def _():
        rinv_ref[...] = part

    @pl.when(j > 0)
    def _():
        rinv_ref[...] += part

    @pl.when(j == pl.num_programs(1) - 1)
    def _():
        rinv_ref[...] = lax.rsqrt(rinv_ref[...] * (1.0 / d_model) + EPS)


def _prep_kernel(x_ref, g_ref, xg_ref, rinv_ref):
    x = x_ref[...]
    xg_ref[...] = (x * g_ref[...]).astype(xg_ref.dtype)
    rinv_ref[...] = lax.rsqrt(jnp.mean(x * x, axis=-1, keepdims=True) + EPS)


def _prep(x, g, casts):
    m, d = x.shape
    tm = _row_tile(m, PREP_ROW_TILE)
    return _call(
        _prep_kernel, "norm_prep", (m // tm,), ("arbitrary",),
        [pl.BlockSpec((tm, d), lambda i: (i, 0)),
         pl.BlockSpec((1, d), lambda i: (0, 0))],
        [x, g.reshape(1, d)],
        [pl.BlockSpec((tm, d), lambda i: (i, 0)),
         pl.BlockSpec((tm, 1), lambda i: (i, 0))],
        [jax.ShapeDtypeStruct((m, d), BF16), jax.ShapeDtypeStruct((m, 1), F32)],
        casts)


def _normed_dot(a_ref, rinv_ref, w_ref):
    return _dot(a_ref[...], w_ref[...]) * rinv_ref[...]


def _mm_kernel(a_ref, rinv_ref, w_ref, o_ref):
    o_ref[...] = _normed_dot(a_ref, rinv_ref, w_ref).astype(o_ref.dtype)


def _matmul_normed(a, rinv, w, col0, n, out_dtype, name, casts):
    m, k = a.shape
    tm, tn = _row_tile(m, ROW_TILE), COL_TILE_PLAIN
    assert n % tn == 0 and col0 % tn == 0
    j0 = col0 // tn
    (out,), cast = _call(
        _mm_kernel, name, (m // tm, n // tn), ("arbitrary", "arbitrary"),
        [pl.BlockSpec((tm, k), lambda i, j: (i, 0)),
         pl.BlockSpec((tm, 1), lambda i, j: (i, 0)),
         pl.BlockSpec((k, tn), lambda i, j: (0, j0 + j))],
        [a, rinv, w],
        [pl.BlockSpec((tm, tn), lambda i, j: (i, j))],
        [jax.ShapeDtypeStruct((m, n), out_dtype)],
        casts)
    return out, cast


def _mm_headnorm_kernel(a_ref, rinv_ref, w_ref, g_ref, o_ref, *, norm_tiles):
    j = pl.program_id(1)

    @pl.when(j < norm_tiles)
    def _():
        for cols in _col_chunks(o_ref.shape[1]):
            acc = _dot(a_ref[...], w_ref[:, cols]) * rinv_ref[...]
            for h in range(acc.shape[1] // HEAD_DIM):
                a = acc[:, h * HEAD_DIM:(h + 1) * HEAD_DIM]
                sl = slice(cols.start + h * HEAD_DIM, cols.start + (h + 1) * HEAD_DIM)
                ms = jnp.mean(a * a, axis=-1, keepdims=True)
                g = g_ref[:, _tile_cols(sl, o_ref.shape[1])]
                o_ref[:, sl] = (a * lax.rsqrt(ms + EPS) * g).astype(o_ref.dtype)

    @pl.when(j >= norm_tiles)
    def _():
        o_ref[...] = _normed_dot(a_ref, rinv_ref, w_ref).astype(o_ref.dtype)


def _matmul_headnorm(a, rinv, w, g_cols, norm_cols, casts):
    m, k = a.shape
    n = g_cols.shape[1]
    tm, tn = _row_tile(m, ROW_TILE), COL_TILE
    assert n % tn == 0 and norm_cols % tn == 0
    (out,), cast = _call(
        functools.partial(_mm_headnorm_kernel, norm_tiles=norm_cols // tn),
        "matmul_headnorm", (m // tm, n // tn), ("arbitrary", "arbitrary"),
        [pl.BlockSpec((tm, k), lambda i, j: (i, 0)),
         pl.BlockSpec((tm, 1), lambda i, j: (i, 0)),
         pl.BlockSpec((k, tn), lambda i, j: (0, j)),
         _resident((1, n), lambda i, j: (0, 0))],
        [a, rinv, w, g_cols],
        [pl.BlockSpec((tm, tn), lambda i, j: (i, j))],
        [jax.ShapeDtypeStruct((m, n), BF16)],
        casts)
    return out, cast


def _mm_residual_kernel(*refs, n_a):
    a_refs = refs[:n_a]
    w_ref, res_ref, gain_ref, x_ref, xg_ref, rinv_ref = refs[n_a:]
    kc = a_refs[0].shape[1]
    x_chunks = []
    for cols in _col_chunks(x_ref.shape[1]):
        acc = res_ref[:, cols]
        for c, a_ref in enumerate(a_refs):
            acc = acc + _dot(a_ref[...], w_ref[c * kc:(c + 1) * kc, cols])
        x_ref[:, cols] = acc
        x_chunks.append((cols, acc))
    _emit_norm_inputs(x_chunks, gain_ref, xg_ref, rinv_ref,
                      x_ref.shape[1] * pl.num_programs(1))


def _mm_addres_kernel(*refs, n_a):
    a_refs = refs[:n_a]
    w_ref, res_ref, x_ref = refs[n_a:]
    kc = a_refs[0].shape[1]
    acc = res_ref[...]
    for c, a_ref in enumerate(a_refs):
        acc = acc + _dot(a_ref[...], w_ref[c * kc:(c + 1) * kc, :])
    x_ref[...] = acc


def _matmul_addres(a_list, w, row_blk, res):
    m, kc = a_list[0].shape
    n_a = len(a_list)
    k, n = kc * n_a, w.shape[1]
    tm, tn = _row_tile(m, ROW_TILE), COL_TILE_PLAIN
    tile = pl.BlockSpec((tm, tn), lambda i, j: (i, j))
    (out,), _ = _call(
        functools.partial(_mm_addres_kernel, n_a=n_a),
        "matmul_addres", (m // tm, n // tn), ("arbitrary", "arbitrary"),
        [pl.BlockSpec((tm, kc), lambda i, j: (i, 0))] * n_a
        + [pl.BlockSpec((k, tn), lambda i, j: (row_blk, j)), tile],
        [*a_list, w, res],
        [tile], [jax.ShapeDtypeStruct((m, n), F32)])
    return out


def _matmul_residual(a_list, w, res, next_gain, casts, row_blk=0):
    m, kc = a_list[0].shape
    n = w.shape[1]
    n_a = len(a_list)
    k = kc * n_a
    assert w.shape[0] % k == 0
    tm, tn = _row_tile(m, ROW_TILE), (COL_TILE if k <= MAX_FULL_TILE_K else COL_TILE_WIDE_K)
    a_spec = pl.BlockSpec((tm, kc), lambda i, j: (i, 0))
    tile = pl.BlockSpec((tm, tn), lambda i, j: (i, j))
    return _call(
        functools.partial(_mm_residual_kernel, n_a=n_a),
        "matmul_residual", (m // tm, n // tn), ("arbitrary", "arbitrary"),
        [a_spec] * n_a + [pl.BlockSpec((k, tn), lambda i, j: (row_blk, j)),
                          tile,
                          _resident((1, n), lambda i, j: (0, 0))],
        [*a_list, w, res, next_gain.reshape(1, n)],
        [tile, tile, pl.BlockSpec((tm, 1), lambda i, j: (i, 0))],
        [jax.ShapeDtypeStruct((m, n), F32),
         jax.ShapeDtypeStruct((m, n), BF16),
         jax.ShapeDtypeStruct((m, 1), F32)],
        casts)


def _ple_update(a_ref, rinv_ref, wg_ref, p_ref, wp_ref, res_ref, x_ref):
    pb = p_ref[...].astype(BF16)
    half_rinv = 0.5 * rinv_ref[...]
    x_chunks = []
    for cols in _col_chunks(x_ref.shape[1]):
        t = jnp.tanh(_dot(a_ref[...], wg_ref[:, cols]) * half_rinv)
        half_proj = _dot(pb, wp_ref[:, _tile_cols(cols, x_ref.shape[1])])
        x_new = res_ref[:, cols] + (half_proj + half_proj * t)
        x_ref[:, cols] = x_new
        x_chunks.append((cols, x_new))
    return x_chunks


def _ple_kernel(a_ref, rinv_ref, wg_ref, p_ref, wp_ref, res_ref, gain_ref,
                x_ref, xg_ref, rinv_out_ref):
    x_chunks = _ple_update(a_ref, rinv_ref, wg_ref, p_ref, wp_ref, res_ref, x_ref)
    _emit_norm_inputs(x_chunks, gain_ref, xg_ref, rinv_out_ref,
                      x_ref.shape[1] * pl.num_programs(1))


def _ple_last_kernel(a_ref, rinv_ref, wg_ref, p_ref, wp_ref, res_ref, x_ref):
    _ple_update(a_ref, rinv_ref, wg_ref, p_ref, wp_ref, res_ref, x_ref)


def _per_layer_embed(x, xg, rinv, p, layer, w_gate, w_proj, next_gain, casts):
    m, k = xg.shape
    n = w_gate.shape[1]
    kp = p.shape[3]
    tm, tn = _row_tile(m, ROW_TILE), COL_TILE
    tile = pl.BlockSpec((tm, tn), lambda i, j: (i, j))
    col = pl.BlockSpec((tm, 1), lambda i, j: (i, 0))
    in_specs = [pl.BlockSpec((tm, k), lambda i, j: (i, 0)),
                col,
                pl.BlockSpec((k, tn), lambda i, j: (0, j)),
                pl.BlockSpec((None, None, tm, kp), lambda i, j: (layer, 0, i, 0)),
                _resident((kp, n), lambda i, j: (0, 0)),
                tile]
    operands = [xg, rinv, w_gate, p, w_proj, x]
    grid = (m // tm, n // tn)
    sem = ("arbitrary", "arbitrary")
    x_shape = jax.ShapeDtypeStruct((m, n), F32)
    if next_gain is None:
        return _call(_ple_last_kernel, "per_layer_embed_last", grid, sem, in_specs, operands,
                     [tile], [x_shape], casts)
    return _call(
        _ple_kernel, "per_layer_embed", grid, sem,
        in_specs + [_resident((1, n), lambda i, j: (0, 0))],
        operands + [next_gain.reshape(1, n)],
        [tile, tile, col],
        [x_shape, jax.ShapeDtypeStruct((m, n), BF16), jax.ShapeDtypeStruct((m, 1), F32)],
        casts)


def _attn_block(sink_ref, bias_ref, q_ref, kv_parts, z_ref, o_ref, rows, outside):
    for kv in range(N_KV):
        ksl = slice(kv * HEAD_DIM, (kv + 1) * HEAD_DIM)
        vsl = slice(KV_W + kv * HEAD_DIM, KV_W + (kv + 1) * HEAD_DIM)
        q4 = jnp.concatenate(
            [q_ref[rows, (kv * GROUP + g) * HEAD_DIM:(kv * GROUP + g + 1) * HEAD_DIM]
             for g in range(GROUP)], axis=0)
        kw = jnp.concatenate([r[rs, ksl] for r, rs in kv_parts], axis=0)
        vw = jnp.concatenate([r[rs, vsl] for r, rs in kv_parts], axis=0)
        s = lax.dot_general(q4, kw, (((1,), (1,)), ((), ())), preferred_element_type=F32)
        inv, probs = [], []
        for g in range(GROUP):
            head = kv * GROUP + g
            sg = s[g * BLOCK:(g + 1) * BLOCK] + bias_ref[head]
            if outside is not None:
                sg = jnp.where(outside, MASKED_SCORE, sg)
            sink2 = sink_ref[head] * LOG2E
            mx = jnp.maximum(jnp.max(sg, axis=-1, keepdims=True), sink2)
            e = jnp.exp2(sg - mx)
            denom = jnp.sum(e, axis=-1, keepdims=True) + jnp.exp2(sink2 - mx)
            inv.append(1.0 / denom)
            probs.append(e.astype(BF16))
        o4 = _dot(jnp.concatenate(probs, axis=0), vw)
        for g in range(GROUP):
            sl = slice((kv * GROUP + g) * HEAD_DIM, (kv * GROUP + g + 1) * HEAD_DIM)
            o_ref[rows, sl] = (o4[g * BLOCK:(g + 1) * BLOCK]
                               * (inv[g] * _silu(z_ref[rows, sl]))).astype(o_ref.dtype)


def _attn_kernel(sink_ref, bias_ref, q_ref, kvp_ref, kvc_ref, kvn_ref, z_ref, o_ref):
    bi = pl.program_id(0)
    is_first = bi == 0
    is_last = bi == pl.num_programs(0) - 1
    on_edge = is_first | is_last
    lo, hi = slice(0, BLOCK), slice(BLOCK, 2 * BLOCK)
    parts_lo = [(kvp_ref, lo), (kvc_ref, lo), (kvc_ref, hi)]
    parts_hi = [(kvc_ref, lo), (kvc_ref, hi), (kvn_ref, lo)]

    def both(outside_lo, outside_hi):
        _attn_block(sink_ref, bias_ref, q_ref, parts_lo, z_ref, o_ref, lo, outside_lo)
        _attn_block(sink_ref, bias_ref, q_ref, parts_hi, z_ref, o_ref, hi, outside_hi)

    @pl.when(jnp.logical_not(on_edge))
    def _():
        both(None, None)

    @pl.when(on_edge)
    def _():
        col = lax.broadcasted_iota(jnp.int32, (BLOCK, 3 * BLOCK), 1)
        both(is_first & (col < BLOCK), is_last & (col >= 2 * BLOCK))


def _attention_bias():
    slopes = 2.0 ** (-8.0 * jnp.arange(1, N_HEADS + 1, dtype=F32) / N_HEADS)
    t = jnp.arange(BLOCK)[:, None]
    c = jnp.arange(3 * BLOCK)[None, :]
    dist = jnp.abs(t + BLOCK - c)
    bias = -(slopes * LOG2E)[:, None, None] * dist.astype(F32)[None]
    return jnp.where((dist <= WINDOW)[None], bias, MASKED_SCORE)


def _attention(qkv, z, sink, casts):
    s_len = qkv.shape[0]
    nb = s_len // BLOCK
    assert nb % 2 == 0
    kv_blk = Q_W // (2 * KV_W)
    assert kv_blk * 2 * KV_W == Q_W
    halo = (BLOCK, 2 * KV_W)
    q_spec = pl.BlockSpec((2 * BLOCK, Q_W), lambda bi: (bi, 0))
    (out,), cast = _call(
        _attn_kernel, "banded_attention", (nb // 2,), ("arbitrary",),
        [pl.BlockSpec(memory_space=pltpu.SMEM),
         _resident((N_HEADS, BLOCK, 3 * BLOCK), lambda bi: (0, 0, 0)),
         q_spec,
         pl.BlockSpec(halo, lambda bi: (jnp.maximum(2 * bi - 1, 0), kv_blk)),
         pl.BlockSpec((2 * BLOCK, 2 * KV_W), lambda bi: (bi, kv_blk)),
         pl.BlockSpec(halo, lambda bi: (jnp.minimum(2 * bi + 2, nb - 1), kv_blk)),
         q_spec],
        [sink, _attention_bias(), qkv, qkv, qkv, qkv, z],
        [q_spec],
        [jax.ShapeDtypeStruct((s_len, Q_W), BF16)],
        casts)
    return out, cast


def _window_sum(e, win, rows):
    n = e.shape[0]
    h = POOL_HALO

    def ahead(x, k):
        return pltpu.roll(x, n - k, axis=0)

    def behind(x, k):
        return pltpu.roll(x, k, axis=0)

    if win == 2:
        return (e + ahead(e, 1))[h:h + rows]
    q1 = e + behind(e, 1)
    if win == 4:
        return (q1 + ahead(q1, 2))[h:h + rows]
    q2 = q1 + behind(q1, 2)
    if win == 8:
        return (q2 + ahead(q2, 4))[h:h + rows]
    assert win == 16
    q3 = q2 + behind(q2, 4)
    return q3[h:h + rows] + q3[2 * h:2 * h + rows]


def _pool_group_kernel(vp_ref, vc_ref, vn_ref, w_ref, scale_ref, z_ref, o_ref, d_ref,
                       *, seq_len, win):
    i = pl.program_id(0)
    n_i = pl.num_programs(0)
    tm = vc_ref.shape[0]
    left = (win - 1) // 2
    right = win - 1 - left
    h = POOL_HALO
    n_chunks = tm // POOL_CHUNK
    for c in range(n_chunks):
        r0 = c * POOL_CHUNK
        top = jnp.where(i > 0, vp_ref[...], 0.0) if c == 0 else vc_ref[r0 - h:r0, :]
        bot = (jnp.where(i < n_i - 1, vn_ref[...], 0.0) if c == n_chunks - 1
               else vc_ref[r0 + POOL_CHUNK:r0 + POOL_CHUNK + h, :])
        cur = vc_ref[r0:r0 + POOL_CHUNK, :]
        e = jnp.concatenate([top, cur, bot], axis=0)
        t = i * tm + r0 + lax.broadcasted_iota(jnp.int32, (POOL_CHUNK, 1), 0)
        lo = jnp.maximum(t - left, 0)
        hi = jnp.minimum(t + right + 1, seq_len)
        inv_cnt = 1.0 / (hi - lo).astype(F32)
        d_ref[r0:r0 + POOL_CHUNK, :] = (
            _window_sum(e, win, POOL_CHUNK) * inv_cnt - cur).astype(d_ref.dtype)
    y = _dot(d_ref[...], w_ref[...].astype(BF16)) * scale_ref[...]
    o_ref[...] = (y * _silu(z_ref[...])).astype(o_ref.dtype)


def _pool_group(u, w_grp, layer, scale, grp):
    s_len = u.shape[0]
    gw = w_grp.shape[2]
    tm = _row_tile(s_len, POOL_ROW_TILE)
    assert tm % POOL_CHUNK == 0
    halo_per_tile = tm // POOL_HALO
    n_halo = s_len // POOL_HALO
    return pl.pallas_call(
        functools.partial(_pool_group_kernel, seq_len=s_len, win=POOL_WINDOWS[grp]),
        grid=(s_len // tm,),
        in_specs=[
            pl.BlockSpec((POOL_HALO, gw), lambda i: (jnp.maximum(i * halo_per_tile - 1, 0), grp)),
            pl.BlockSpec((tm, gw), lambda i: (i, grp)),
            pl.BlockSpec((POOL_HALO, gw),
                         lambda i: (jnp.minimum((i + 1) * halo_per_tile, n_halo - 1), grp)),
            _resident((None, None, gw, gw), lambda i: (layer, grp, 0, 0)),
            pl.BlockSpec((1, gw), lambda i: (0, grp)),
            pl.BlockSpec((tm, gw), lambda i: (i, N_POOL_GROUPS + grp)),
        ],
        out_specs=pl.BlockSpec((tm, gw), lambda i: (i, 0)),
        out_shape=jax.ShapeDtypeStruct((s_len, gw), BF16),
        scratch_shapes=[pltpu.VMEM((tm, gw), BF16)],
        compiler_params=_params("arbitrary"),
        name=f"pool_group{grp}",
    )(u, u, u, w_grp, scale.reshape(1, -1), u)


def kernel(x, p, norm_g, attn_w_in, attn_q_norm_g, attn_k_norm_g, attn_sink, attn_w_out,
           pool_w_in, pool_w_grp, pool_scale, pool_w_out, ple_norm_g, ple_w_gate, ple_w_proj):
    b, s_len, d = x.shape
    assert b == 1
    depth = p.shape[0]
    assert depth % 2 == 0
    xf = x.reshape(s_len, d)
    (xg, rinv), (w_in,) = _prep(xf, norm_g[0], [(attn_w_in, (0,), 1.0)])
    for pair in range(depth // 2):
        i = 2 * pair
        g_cols = jnp.concatenate(
            [jnp.tile(attn_q_norm_g[pair].astype(F32) * SCORE_SCALE, N_HEADS),
             jnp.tile(attn_k_norm_g[pair].astype(F32), N_KV),
             jnp.ones((KV_W,), F32)]).reshape(1, QKV_W)
        qkv, (w_out, w_pool_in) = _matmul_headnorm(
            xg, rinv, w_in, g_cols, Q_W + KV_W,
            [(attn_w_out, (pair,), 1.0), (pool_w_in, (pair,), 1.0)])
        z, (w_gate, w_proj) = _matmul_normed(
            xg, rinv, w_in, QKV_W, Q_W, F32, "matmul_attn_gate",
            [(ple_w_gate, (i,), 1.0), (ple_w_proj, (i,), 0.5)])
        gated, _ = _attention(qkv, z, attn_sink[pair].astype(F32), [])
        (xf, xg, rinv), _ = _matmul_residual([gated], w_out, xf, ple_norm_g[i], [])
        (xf, xg, rinv), _ = _per_layer_embed(xf, xg, rinv, p, i, w_gate, w_proj,
                                             norm_g[i + 1], [])
        last = i + 2 == depth
        u, (w_gate, w_proj, w_pool_out, *w_next) = _matmul_normed(
            xg, rinv, w_pool_in, 0, w_pool_in.shape[1], F32, "matmul_pool_in",
            [(ple_w_gate, (i + 1,), 1.0), (ple_w_proj, (i + 1,), 0.5),
             (pool_w_out, (pair,), 1.0)]
            + ([] if last else [(attn_w_in, (pair + 1,), 1.0)]))
        gated = [_pool_group(u, pool_w_grp, pair, pool_scale[pair], grp)
                 for grp in range(N_POOL_GROUPS)]
        half = N_POOL_GROUPS // 2
        xf = _matmul_addres(gated[:half], w_pool_out, 0, xf)
        (xf, xg, rinv), _ = _matmul_residual(gated[half:], w_pool_out, xf, ple_norm_g[i + 1],
                                             [], row_blk=1)
        outs, _ = _per_layer_embed(xf, xg, rinv, p, i + 1, w_gate, w_proj,
                                   None if last else norm_g[i + 2], [])
        if last:
            (xf,) = outs
        else:
            xf, xg, rinv = outs
            (w_in,) = w_next
    return xf.reshape(b, s_len, d)
```
